```python
import functools
import jax, jax.numpy as jnp
from jax import lax
import numpy as np

D_MODEL = 2048
BATCH = 4
SEQ = 2048
DEPTH = 4
DEC_BATCH = 8
DEC_SEQ = 1
PAST_LEN = 16384
PAGE_SIZE = 128

F32 = jnp.float32
RET_HEADS = 4
RET_DIM = 128
RET_WIDTH = RET_HEADS * RET_DIM
RET_CHUNK = 128
RET_THETA = 10000.0
ATT_HEADS = 8
ATT_DIM = 128
ATT_WIDTH = ATT_HEADS * ATT_DIM
ROT_DIM = ATT_DIM // 4
ROPE_THETA = 500000.0
IDX_HEADS = 16
IDX_DIM = 64
IDX_ROT_DIM = IDX_DIM // 4
TOPK_MAX = 256
Q_BLOCK = 64
ATT_SCALE = ATT_DIM ** -0.5
IDX_W_SCALE = (IDX_HEADS * IDX_DIM) ** -0.5
CONV_WIDTH = 512
CONV_K = 3
D_FF = 5632
FFN_K = 3
ALPHA = (2 * DEPTH) ** 0.25
BETA = (8 * DEPTH) ** -0.25
LN_EPS = 1e-5

MIX_WIDTH = RET_WIDTH + ATT_WIDTH + CONV_WIDTH
IN_SIZES = (RET_WIDTH, RET_WIDTH, RET_WIDTH, RET_WIDTH,
            ATT_WIDTH, ATT_WIDTH, ATT_WIDTH,
            IDX_HEADS * IDX_DIM, IDX_DIM, IDX_HEADS,
            CONV_WIDTH, CONV_WIDTH, CONV_WIDTH)
IN_OFFSETS = tuple(sum(IN_SIZES[:i + 1]) for i in range(len(IN_SIZES) - 1))
D_IN = sum(IN_SIZES)
VALUE_SEGMENTS = (2, 6, 12)

kernel_name = 'hymba_retention_dsa_shortconv_deepnorm_step'


def layer_norm(x, g, b):
    xf = x.astype(F32)
    mu = xf.mean(-1, keepdims=True)
    var = jnp.square(xf - mu).mean(-1, keepdims=True)
    return ((xf - mu) * lax.rsqrt(var + LN_EPS) * g.astype(F32) + b.astype(F32)).astype(x.dtype)


def rope(x, pos, rot_dim, theta):
    half = rot_dim // 2
    inv = theta ** (-jnp.arange(half, dtype=F32) / half)
    ang = pos.astype(F32)[:, None] * inv[None, :]
    cos = jnp.cos(ang)[:, None, :]
    sin = jnp.sin(ang)[:, None, :]
    xr = x[..., :rot_dim].astype(F32)
    x1, x2 = xr[..., :half], xr[..., half:]
    rot = jnp.concatenate([x1 * cos - x2 * sin, x2 * cos + x1 * sin], axis=-1).astype(x.dtype)
    return jnp.concatenate([rot, x[..., rot_dim:]], axis=-1)


def gather_rows(a, idx):
    return jax.vmap(lambda ab, ib: ab[ib])(a, idx)


def causal_dwconv(u, prefix, w):
    K = w.shape[0]
    L = u.shape[1]
    up = jnp.concatenate([prefix.astype(u.dtype), u], axis=1)
    y = w[0] * up[:, 0:L]
    for j in range(1, K):
        y = y + w[j] * up[:, j:j + L]
    return y, up[:, L:]


def retention(q, k, v, state0):
    B, L, H, D = q.shape
    C = RET_CHUNK if L % RET_CHUNK == 0 else L
    n = L // C
    lg = jnp.log(1.0 - 2.0 ** (-5.0 - jnp.arange(H, dtype=F32)))
    i = jnp.arange(C, dtype=F32)
    diff = i[:, None] - i[None, :]
    decay = jnp.where(diff >= 0, jnp.exp(lg[:, None, None] * jnp.maximum(diff, 0.0)), 0.0)
    xi = jnp.exp(lg[:, None] * (i + 1.0))[:, :, None]
    zeta = jnp.exp(lg[:, None] * (C - 1.0 - i))[:, :, None]
    g_c = jnp.exp(lg * C)[:, None, None]

    def to_chunks(t):
        return t.astype(F32).reshape(B, n, C, H, D).transpose(1, 0, 3, 2, 4)

    def step(R, qkv):
        qc, kc, vc = qkv
        inner = jnp.einsum('bhid,bhjd->bhij', qc, kc) * decay
        o = (jnp.einsum('bhij,bhjd->bhid', inner, vc)
             + jnp.einsum('bhid,bhde->bhie', qc, R) * xi)
        R = R * g_c + jnp.einsum('bhjd,bhje->bhde', kc * zeta, vc)
        return R, o

    R, o = lax.scan(step, state0.astype(F32), (to_chunks(q), to_chunks(k), to_chunks(v)))
    o = o.transpose(1, 0, 3, 2, 4).reshape(B, L, H, D)
    return o, R.astype(state0.dtype)


def head_norm(o, g):
    mu = o.mean(-1, keepdims=True)
    var = jnp.square(o - mu).mean(-1, keepdims=True)
    return (o - mu) * lax.rsqrt(var + LN_EPS) * g.astype(F32).reshape(o.shape[-2:])


def index_select(qi, wi, ki, key_pos, tpos, topk):
    dots = jnp.einsum('bthd,bsd->bths', qi.astype(F32), ki)
    score = jnp.einsum('bths,bth->bts', jax.nn.relu(dots), wi.astype(F32))
    causal = key_pos[None, :] <= tpos[:, None]
    score = jnp.where(causal[None], score, -jnp.inf)
    _, sel = lax.top_k(score, topk)
    valid = sel <= tpos[None, :, None]
    return sel, valid


def sparse_attend(q, ks, vs, valid):
    s = jnp.einsum('bthd,btkhd->bthk', q.astype(F32), ks.astype(F32)) * ATT_SCALE
    s = jnp.where(valid[:, :, None, :], s, -jnp.inf)
    p = jax.nn.softmax(s, axis=-1)
    return jnp.einsum('bthk,btkhd->bthd', p.astype(vs.dtype), vs).astype(q.dtype)


def dsa_prompt(q, k, v, qi, ki, wi):
    B, S, H, D = q.shape
    topk = min(TOPK_MAX, S // 4)
    nb = S // Q_BLOCK
    kf = ki.astype(F32)
    key_pos = jnp.arange(S, dtype=jnp.int32)

    def blocks(t):
        return t.reshape((B, nb, Q_BLOCK) + t.shape[2:]).swapaxes(0, 1)

    def one_block(args):
        qb, qib, wib, t0 = args
        tpos = t0 + jnp.arange(Q_BLOCK, dtype=jnp.int32)
        sel, valid = index_select(qib, wib, kf, key_pos, tpos, topk)
        return sparse_attend(qb, gather_rows(k, sel), gather_rows(v, sel), valid)

    t0s = jnp.arange(nb, dtype=jnp.int32) * Q_BLOCK
    out = lax.map(one_block, (blocks(q), blocks(qi), blocks(wi), t0s))
    return out.swapaxes(0, 1).reshape(B, S, H, D)


def dsa_sample(q, k, v, qi, ki, wi, *, cache_k, cache_v, cache_idx_k, page_table, layer):
    DB, T = q.shape[:2]
    past = page_table.shape[1] * PAGE_SIZE
    L = past + T
    topk = min(TOPK_MAX, L // 4)
    ki_past = cache_idx_k[layer, page_table].reshape(DB, past, IDX_DIM)
    ki_all = jnp.concatenate([ki_past.astype(F32), ki.astype(F32)], axis=1)
    tpos = past + jnp.arange(T, dtype=jnp.int32)
    sel, valid = index_select(qi, wi, ki_all, jnp.arange(L, dtype=jnp.int32), tpos, topk)
    in_past = (sel < past)[..., None, None]
    sp = jnp.minimum(sel, past - 1)
    phys = gather_rows(page_table, sp // PAGE_SIZE)
    off = sp % PAGE_SIZE
    sn = jnp.clip(sel - past, 0, T - 1)
    ks = jnp.where(in_past, cache_k[layer, phys, off], gather_rows(k, sn))
    vs = jnp.where(in_past, cache_v[layer, phys, off], gather_rows(v, sn))
    return sparse_attend(q, ks, vs, valid)


def mixer(x, pos, ret_state0, conv_prefix, attend, w_in, ret_gn_g, idx_kn_g, idx_kn_b, conv_w, w_out):
    B, L, _ = x.shape
    z = jnp.einsum('bld,de->ble', x, w_in)
    (rq, rk, rv, rg, aq, ak, av, iq, ik, iw, cb, cc, ch) = jnp.split(z, IN_OFFSETS, axis=-1)
    heads = lambda t, h, d: t.reshape(B, L, h, d)
    rq = rope(heads(rq, RET_HEADS, RET_DIM), pos, RET_DIM, RET_THETA)
    rk = rope(heads(rk, RET_HEADS, RET_DIM), pos, RET_DIM, RET_THETA) * (RET_DIM ** -0.5)
    ro, ret_state = retention(rq, rk, heads(rv, RET_HEADS, RET_DIM), ret_state0)
    ro = head_norm(ro, ret_gn_g).reshape(B, L, RET_WIDTH).astype(x.dtype) * jax.nn.silu(rg)
    aq = rope(heads(aq, ATT_HEADS, ATT_DIM), pos, ROT_DIM, ROPE_THETA)
    ak = rope(heads(ak, ATT_HEADS, ATT_DIM), pos, ROT_DIM, ROPE_THETA)
    av = heads(av, ATT_HEADS, ATT_DIM)
    iq = rope(heads(iq, IDX_HEADS, IDX_DIM), pos, IDX_ROT_DIM, ROPE_THETA)
    ik = rope(layer_norm(ik, idx_kn_g, idx_kn_b)[:, :, None, :], pos, IDX_ROT_DIM, ROPE_THETA)[:, :, 0, :]
    iw = iw * IDX_W_SCALE
    ao = attend(aq, ak, av, iq, ik, iw).reshape(B, L, ATT_WIDTH)
    cu, conv_state = causal_dwconv(cc * ch, conv_prefix, conv_w)
    co = cb * cu
    y = jnp.einsum('ble,ed->bld', jnp.concatenate([ro, ao, co], axis=-1), w_out)
    return y, ak, av, ik, ret_state, conv_state


def conv_ffn(x, prefix, w_up, cw, w_down):
    h = jnp.einsum('bld,df->blf', x, w_up)
    h, st = causal_dwconv(h, prefix, cw)
    a, b = jnp.split(h, 2, axis=-1)
    return jnp.einsum('blf,fd->bld', jax.nn.silu(a) * b, w_down), st


def decoder_layer(x, pos, ret_state0, conv_prefix, ffn_prefix, attend, params):
    (w_in, ret_gn_g, idx_kn_g, idx_kn_b, conv_w, w_out,
     ln1_g, ln1_b, w_up, ffn_conv_w, w_down, ln2_g, ln2_b) = params
    y, ak, av, ik, rs, cs = mixer(x, pos, ret_state0, conv_prefix, attend,
                                  w_in, ret_gn_g, idx_kn_g, idx_kn_b, conv_w, w_out)
    x = layer_norm(ALPHA * x + y, ln1_g, ln1_b)
    f, fs = conv_ffn(x, ffn_prefix, w_up, ffn_conv_w, w_down)
    x = layer_norm(ALPHA * x + f, ln2_g, ln2_b)
    return x, (ak, av, ik, rs, cs, fs)


def setup_inputs(seed: int = 0) -> dict:
    key = jax.random.key(seed)
    ks = jax.random.split(key, 32)
    n_pages = PAST_LEN // PAGE_SIZE
    n_pool = (DEC_BATCH * n_pages * 5) // 4
    nrm = lambda k, shape, scale=1.0: scale * jax.random.normal(k, shape, F32)
    x_prompt = nrm(ks[0], (BATCH, SEQ, D_MODEL))
    x_sample = nrm(ks[1], (DEC_BATCH, DEC_SEQ, D_MODEL))
    cache_k = nrm(ks[2], (DEPTH, n_pool, PAGE_SIZE, ATT_HEADS, ATT_DIM))
    cache_v = nrm(ks[3], (DEPTH, n_pool, PAGE_SIZE, ATT_HEADS, ATT_DIM), BETA)
    cache_idx_k = nrm(ks[4], (DEPTH, n_pool, PAGE_SIZE, IDX_DIM))
    state_ret = nrm(ks[5], (DEPTH, DEC_BATCH, RET_HEADS, RET_DIM, RET_DIM), 0.3)
    state_conv = nrm(ks[6], (DEPTH, DEC_BATCH, CONV_K - 1, CONV_WIDTH), BETA)
    state_ffn = nrm(ks[7], (DEPTH, DEC_BATCH, FFN_K - 1, 2 * D_FF))
    perm = jax.random.permutation(ks[8], n_pool)[:DEC_BATCH * n_pages]
    page_table = perm.reshape(DEC_BATCH, n_pages).astype(jnp.int32)
    col_scale = jnp.concatenate([jnp.full((n,), BETA if i in VALUE_SEGMENTS else 1.0, F32)
                                 for i, n in enumerate(IN_SIZES)])
    w_in = nrm(ks[9], (DEPTH, D_MODEL, D_IN), D_MODEL ** -0.5) * col_scale
    ret_gn_g = 1.0 + nrm(ks[10], (DEPTH, RET_WIDTH), 0.02)
    idx_kn_g = 1.0 + nrm(ks[11], (DEPTH, IDX_DIM), 0.02)
    idx_kn_b = nrm(ks[12], (DEPTH, IDX_DIM), 0.02)
    conv_w = nrm(ks[13], (DEPTH, CONV_K, CONV_WIDTH), CONV_K ** -0.5)
    w_out = nrm(ks[14], (DEPTH, MIX_WIDTH, D_MODEL), BETA * MIX_WIDTH ** -0.5)
    ln1_g = 1.0 + nrm(ks[15], (DEPTH, D_MODEL), 0.02)
    ln1_b = nrm(ks[16], (DEPTH, D_MODEL), 0.02)
    w_up = nrm(ks[17], (DEPTH, D_MODEL, 2 * D_FF), D_MODEL ** -0.5)
    ffn_conv_w = nrm(ks[18], (DEPTH, FFN_K, 2 * D_FF), FFN_K ** -0.5)
    w_down = nrm(ks[19], (DEPTH, D_FF, D_MODEL), BETA * D_FF ** -0.5)
    ln2_g = 1.0 + nrm(ks[20], (DEPTH, D_MODEL), 0.02)
    ln2_b = nrm(ks[21], (DEPTH, D_MODEL), 0.02)
    return {'x_prompt': x_prompt, 'x_sample': x_sample,
            'cache_k': cache_k, 'cache_v': cache_v, 'cache_idx_k': cache_idx_k,
            'state_ret': state_ret, 'state_conv': state_conv, 'state_ffn': state_ffn,
            'page_table': page_table,
            'w_in': w_in, 'ret_gn_g': ret_gn_g, 'idx_kn_g': idx_kn_g, 'idx_kn_b': idx_kn_b,
            'conv_w': conv_w, 'w_out': w_out, 'ln1_g': ln1_g, 'ln1_b': ln1_b,
            'w_up': w_up, 'ffn_conv_w': ffn_conv_w, 'w_down': w_down,
            'ln2_g': ln2_g, 'ln2_b': ln2_b}


def reference(x_prompt, x_sample, cache_k, cache_v, cache_idx_k, state_ret, state_conv, state_ffn,
              page_table, w_in, ret_gn_g, idx_kn_g, idx_kn_b, conv_w, w_out, ln1_g, ln1_b,
              w_up, ffn_conv_w, w_down, ln2_g, ln2_b):
    B, S, _ = x_prompt.shape
    DB, T, _ = x_sample.shape
    past = page_table.shape[1] * PAGE_SIZE
    pos_p = jnp.arange(S, dtype=jnp.int32)
    pos_s = past + jnp.arange(T, dtype=jnp.int32)
    ret0_p = jnp.zeros((B, RET_HEADS, RET_DIM, RET_DIM), x_prompt.dtype)
    conv0_p = jnp.zeros((B, CONV_K - 1, CONV_WIDTH), x_prompt.dtype)
    ffn0_p = jnp.zeros((B, FFN_K - 1, 2 * D_FF), x_prompt.dtype)
    hp, hs = x_prompt, x_sample
    new_p, new_s = [], []
    for l in range(DEPTH):
        params = (w_in[l], ret_gn_g[l], idx_kn_g[l], idx_kn_b[l], conv_w[l], w_out[l],
                  ln1_g[l], ln1_b[l], w_up[l], ffn_conv_w[l], w_down[l], ln2_g[l], ln2_b[l])
        hp, st_p = decoder_layer(hp, pos_p, ret0_p, conv0_p, ffn0_p, dsa_prompt, params)
        new_p.append(st_p)
        attend_s = functools.partial(dsa_sample, cache_k=cache_k, cache_v=cache_v,
                                     cache_idx_k=cache_idx_k, page_table=page_table, layer=l)
        hs, st_s = decoder_layer(hs, pos_s, state_ret[l], state_conv[l], state_ffn[l], attend_s, params)
        new_s.append(st_s)
    n_pp = S // PAGE_SIZE
    p_k = jnp.stack([s[0] for s in new_p]).reshape(DEPTH, B, n_pp, PAGE_SIZE, ATT_HEADS, ATT_DIM)
    p_v = jnp.stack([s[1] for s in new_p]).reshape(DEPTH, B, n_pp, PAGE_SIZE, ATT_HEADS, ATT_DIM)
    p_ik = jnp.stack([s[2] for s in new_p]).reshape(DEPTH, B, n_pp, PAGE_SIZE, IDX_DIM)
    p_ret = jnp.stack([s[3] for s in new_p])
    p_conv = jnp.stack([s[4] for s in new_p])
    p_ffn = jnp.stack([s[5] for s in new_p])
    s_k = jnp.stack([s[0] for s in new_s])
    s_v = jnp.stack([s[1] for s in new_s])
    s_ik = jnp.stack([s[2] for s in new_s])
    s_ret = jnp.stack([s[3] for s in new_s])
    s_conv = jnp.stack([s[4] for s in new_s])
    s_ffn = jnp.stack([s[5] for s in new_s])
    return (hp, hs, p_k, p_v, p_ik, p_ret, p_conv, p_ffn, s_k, s_v, s_ik, s_ret, s_conv, s_ffn)
```

```python
import functools

import numpy as np
import jax
import jax.numpy as jnp
from jax import lax
from jax.experimental import pallas as pl
from jax.experimental.pallas import tpu as pltpu

F32 = jnp.float32
BF16 = jnp.bfloat16
I32 = jnp.int32

PAGE_SIZE = 128
RET_HEADS, RET_DIM, RET_CHUNK, RET_THETA = 4, 128, 128, 10000.0
RET_WIDTH = RET_HEADS * RET_DIM
ATT_HEADS, ATT_DIM = 8, 128
ATT_WIDTH = ATT_HEADS * ATT_DIM
ROT_DIM = ATT_DIM // 4
ROPE_THETA = 500000.0
IDX_HEADS, IDX_DIM = 16, 64
IDX_WIDTH = IDX_HEADS * IDX_DIM
IDX_ROT_DIM = IDX_DIM // 4
TOPK_MAX = 256
ATT_SCALE = ATT_DIM ** -0.5
IDX_W_SCALE = (IDX_HEADS * IDX_DIM) ** -0.5
CONV_WIDTH, CONV_K = 512, 3
FFN_K = 3
LN_EPS = 1e-5

_O_RET = 0
_O_ATT = 4 * RET_WIDTH
_O_IQ = _O_ATT + 3 * ATT_WIDTH
_O_IK = _O_IQ + IDX_WIDTH
_O_IW = _O_IK + IDX_DIM
_O_CONV = _O_IW + IDX_HEADS
_D_IN = _O_CONV + 3 * CONV_WIDTH
_Z_CONV = _O_IK
_Z_MAIN = _Z_CONV + 3 * CONV_WIDTH
LANE = 128
_SMALL_W = LANE

_NEG_BIG = -1e30
_INT_MIN = -2 ** 31
_KEY_NEG_INF = int(np.int32(np.uint32(0xFF800000 ^ 0x7FFFFFFF)))

_VMEM_LIMIT = 56 * 1024 * 1024


def _params(sem):
    return pltpu.CompilerParams(dimension_semantics=sem, vmem_limit_bytes=_VMEM_LIMIT)


def _tile(dim, pref, mult):
    t = min(pref, dim)
    t -= t % mult
    while t >= mult:
        if dim % t == 0:
            return t
        t -= mult
    return dim


def _float_key(s):
    b = lax.bitcast_convert_type(s, I32)
    return b ^ ((b >> 31) & jnp.int32(0x7FFFFFFF))


def _layer_norm_rows(r, g, b):
    mu = jnp.mean(r, axis=-1, keepdims=True)
    d = r - mu
    var = jnp.mean(d * d, axis=-1, keepdims=True)
    return d * lax.rsqrt(var + LN_EPS) * g + b


def _silu(x):
    return x * (1.0 / (1.0 + jnp.exp(-x)))


def _dot(a, b):
    return jnp.dot(a, b, preferred_element_type=F32)


def _dot_nt(a, b):
    return lax.dot_general(a, b, (((1,), (1,)), ((), ())), preferred_element_type=F32)


def _mm_kernel(x_ref, w_ref, o_ref):
    o_ref[...] = _dot(x_ref[...].astype(BF16), w_ref[...]).astype(o_ref.dtype)


def _matmul(x, w, *, tm_pref=1024, tn_pref=512, out_dtype=F32, name="mm"):
    M, K = x.shape
    N = w.shape[1]
    tm = _tile(M, tm_pref, 8)
    tn = _tile(N, tn_pref, LANE)
    return pl.pallas_call(
        _mm_kernel,
        out_shape=jax.ShapeDtypeStruct((M, N), out_dtype),
        grid=(M // tm, N // tn),
        in_specs=[pl.BlockSpec((tm, K), lambda i, j: (i, 0)),
                  pl.BlockSpec((K, tn), lambda i, j: (0, j))],
        out_specs=pl.BlockSpec((tm, tn), lambda i, j: (i, j)),
        compiler_params=_params(("parallel", "arbitrary")),
        name=name,
    )(x, w)


def _inproj_kernel(x_ref, w_ref, ws_ref, o_ref, os_ref):
    x = x_ref[...].astype(BF16)
    o_ref[...] = _dot(x, w_ref[...])

    @pl.when(pl.program_id(1) == 0)
    def _():
        os_ref[...] = _dot(x, ws_ref[...])


def _inproj(x, w_main, w_small, *, tm_pref=1024, tn_pref=512):
    M, K = x.shape
    N = w_main.shape[1]
    tm = _tile(M, tm_pref, 8)
    tn = _tile(N, tn_pref, LANE)
    return pl.pallas_call(
        _inproj_kernel,
        out_shape=(jax.ShapeDtypeStruct((M, N), F32), jax.ShapeDtypeStruct((M, _SMALL_W), F32)),
        grid=(M // tm, N // tn),
        in_specs=[pl.BlockSpec((tm, K), lambda i, j: (i, 0)),
                  pl.BlockSpec((K, tn), lambda i, j: (0, j)),
                  pl.BlockSpec((K, _SMALL_W), lambda i, j: (0, 0))],
        out_specs=(pl.BlockSpec((tm, tn), lambda i, j: (i, j)),
                   pl.BlockSpec((tm, _SMALL_W), lambda i, j: (i, 0))),
        compiler_params=_params(("parallel", "arbitrary")),
        name="inproj",
    )(x, w_main, w_small)


def _outproj_ln_kernel(alpha, ro_ref, ao_ref, co_ref, w_ref, x_ref, g_ref, b_ref, o_ref, ob_ref):
    y = _dot(ro_ref[...].astype(BF16), w_ref[0:RET_WIDTH, :])
    y += _dot(ao_ref[...].astype(BF16), w_ref[RET_WIDTH:RET_WIDTH + ATT_WIDTH, :])
    y += _dot(co_ref[...].astype(BF16), w_ref[RET_WIDTH + ATT_WIDTH:, :])
    out = _layer_norm_rows(alpha * x_ref[...] + y, g_ref[...], b_ref[...])
    o_ref[...] = out
    ob_ref[...] = out.astype(ob_ref.dtype)


def _outproj_ln(ro, ao, co, w_out, x, g, b, alpha, *, tm_pref=256, lowp_dtype=BF16):
    M, D = x.shape
    tm = _tile(M, tm_pref, 8)
    row = lambda i: (i, 0)
    fixed = lambda i: (0, 0)
    return pl.pallas_call(
        functools.partial(_outproj_ln_kernel, alpha),
        out_shape=(jax.ShapeDtypeStruct((M, D), F32), jax.ShapeDtypeStruct((M, D), lowp_dtype)),
        grid=(M // tm,),
        in_specs=[pl.BlockSpec((tm, RET_WIDTH), row), pl.BlockSpec((tm, ATT_WIDTH), row),
                  pl.BlockSpec((tm, CONV_WIDTH), row), pl.BlockSpec(w_out.shape, fixed),
                  pl.BlockSpec((tm, D), row), pl.BlockSpec((1, D), fixed), pl.BlockSpec((1, D), fixed)],
        out_specs=(pl.BlockSpec((tm, D), row), pl.BlockSpec((tm, D), row)),
        compiler_params=_params(("parallel",)),
        name="outproj_ln",
    )(ro, ao, co, w_out, x, g, b)


def _down_ln_kernel(alpha, nk, gt_ref, w_ref, x_ref, g_ref, b_ref, o_ref, ob_ref, acc_ref):
    k = pl.program_id(1)

    @pl.when(k == 0)
    def _():
        acc_ref[...] = jnp.zeros_like(acc_ref)

    acc_ref[...] += _dot(gt_ref[...].astype(BF16), w_ref[...])

    @pl.when(k == nk - 1)
    def _():
        out = _layer_norm_rows(alpha * x_ref[...] + acc_ref[...], g_ref[...], b_ref[...])
        o_ref[...] = out
        ob_ref[...] = out.astype(ob_ref.dtype)


def _down_ln(gt, w_down, x, g, b, alpha, *, tm_pref=512, tk_pref=512, lowp_dtype=BF16):
    M, D = x.shape
    K = gt.shape[1]
    tm = _tile(M, tm_pref, 8)
    tk = _tile(K, tk_pref, LANE)
    nk = K // tk
    return pl.pallas_call(
        functools.partial(_down_ln_kernel, alpha, nk),
        out_shape=(jax.ShapeDtypeStruct((M, D), F32), jax.ShapeDtypeStruct((M, D), lowp_dtype)),
        grid=(M // tm, nk),
        in_specs=[pl.BlockSpec((tm, tk), lambda i, k: (i, k)),
                  pl.BlockSpec((tk, D), lambda i, k: (k, 0)),
                  pl.BlockSpec((tm, D), lambda i, k: (i, 0)),
                  pl.BlockSpec((1, D), lambda i, k: (0, 0)),
                  pl.BlockSpec((1, D), lambda i, k: (0, 0))],
        out_specs=(pl.BlockSpec((tm, D), lambda i, k: (i, 0)),
                   pl.BlockSpec((tm, D), lambda i, k: (i, 0))),
        scratch_shapes=[pltpu.VMEM((tm, D), F32)],
        compiler_params=_params(("parallel", "arbitrary")),
        name="down_ln",
    )(gt, w_down, x, g, b)


def _rope_tables(pos, rot_dim, theta, period):
    half = rot_dim // 2
    inv = theta ** (-jnp.arange(half, dtype=F32) / half)
    ang = pos.astype(F32)[:, None] * inv[None, :]
    cos, sin = jnp.cos(ang), jnp.sin(ang)
    L = pos.shape[0]
    rest = period - rot_dim
    c = jnp.concatenate([cos, cos, jnp.ones((L, rest), F32)], axis=1)
    s_up = jnp.concatenate([-sin, jnp.zeros((L, half + rest), F32)], axis=1)
    s_dn = jnp.concatenate([jnp.zeros((L, half), F32), sin, jnp.zeros((L, rest), F32)], axis=1)
    rep = LANE // period
    return tuple(jnp.tile(t, (1, rep)) for t in (c, s_up, s_dn))


def _rope128(x, c, s_up, s_dn, half):
    if 2 * half == LANE:
        return x * c + pltpu.roll(x, half, axis=1) * (s_up + s_dn)
    return x * c + pltpu.roll(x, LANE - half, axis=1) * s_up + pltpu.roll(x, half, axis=1) * s_dn


def _prep_kernel(aq_ref, ak_ref, av_ref, iq_ref, sm_ref, ac_ref, au_ref, ad_ref,
                 ic_ref, iu_ref, id_ref, kg_ref, kb_ref,
                 q_o, k_o, v_o, kf_o, iq_o, ik_o, sm_o):
    ac, au, ad = ac_ref[...], au_ref[...], ad_ref[...]
    ic, iu, idn = ic_ref[...], iu_ref[...], id_ref[...]
    for h in range(ATT_HEADS):
        sl = slice(h * ATT_DIM, (h + 1) * ATT_DIM)
        q_o[:, sl] = _rope128(aq_ref[:, sl], ac, au, ad, ROT_DIM // 2).astype(q_o.dtype)
        kr = _rope128(ak_ref[:, sl], ac, au, ad, ROT_DIM // 2)
        kf_o[:, sl] = kr
        k_o[:, sl] = kr.astype(k_o.dtype)
    v_o[...] = av_ref[...].astype(v_o.dtype)
    for c in range(IDX_WIDTH // LANE):
        sl = slice(c * LANE, (c + 1) * LANE)
        iq_o[:, sl] = _rope128(iq_ref[:, sl], ic, iu, idn, IDX_ROT_DIM // 2).astype(iq_o.dtype)
    sm = sm_ref[...]
    lane = lax.broadcasted_iota(I32, sm.shape, 1)
    is_k = lane < IDX_DIM
    mu = jnp.sum(jnp.where(is_k, sm, 0.0), axis=-1, keepdims=True) * (1.0 / IDX_DIM)
    d = jnp.where(is_k, sm - mu, 0.0)
    var = jnp.sum(d * d, axis=-1, keepdims=True) * (1.0 / IDX_DIM)
    n = d * lax.rsqrt(var + LN_EPS) * kg_ref[...] + kb_ref[...]
    roped = _rope128(n, ic, iu, idn, IDX_ROT_DIM // 2)
    out = jnp.where(is_k, roped, jnp.where(lane < IDX_DIM + IDX_HEADS, sm * IDX_W_SCALE, 0.0))
    sm_o[...] = out
    ik_o[...] = out.astype(ik_o.dtype)


def _prep(z, zs, att_tabs, idx_tabs, kg, kb, seq, *, tm_pref=256, lowp_dtype=BF16):
    M = z.shape[0]
    tm = _tile(seq, tm_pref, 8)
    nt = seq // tm
    col = lambda c: (lambda i: (i, c))
    tab = lambda i: (i % nt, 0)
    fixed = lambda i: (0, 0)
    W = ATT_WIDTH
    big = lambda dt: jax.ShapeDtypeStruct((M, W), dt)
    return pl.pallas_call(
        _prep_kernel,
        out_shape=(big(lowp_dtype), big(lowp_dtype), big(lowp_dtype), big(F32), big(lowp_dtype),
                   jax.ShapeDtypeStruct((M, LANE), lowp_dtype), jax.ShapeDtypeStruct((M, LANE), F32)),
        grid=(M // tm,),
        in_specs=[pl.BlockSpec((tm, W), col(_O_ATT // W)), pl.BlockSpec((tm, W), col(_O_ATT // W + 1)),
                  pl.BlockSpec((tm, W), col(_O_ATT // W + 2)), pl.BlockSpec((tm, W), col(_O_IQ // W)),
                  pl.BlockSpec((tm, LANE), col(0))]
        + [pl.BlockSpec((tm, LANE), tab)] * 6
        + [pl.BlockSpec((1, LANE), fixed)] * 2,
        out_specs=tuple([pl.BlockSpec((tm, W), col(0))] * 5 + [pl.BlockSpec((tm, LANE), col(0))] * 2),
        compiler_params=_params(("parallel",)),
        name="prep",
    )(z, z, z, z, zs, *att_tabs, *idx_tabs, kg, kb)


def _dsa_kernel(topk, tq, tk, iq_ref, ik_ref, sm_ref, q_ref, k_ref, v_ref, o_ref,
                key_sc, m_sc, l_sc, acc_sc):
    i = pl.program_id(1)
    nkb = i * (tq // tk) + (tq // tk)
    row = i * tq + lax.broadcasted_iota(I32, (tq, tk), 0)

    def score_block(kb, carry):
        k0 = pl.multiple_of(kb * tk, tk)
        ikb = ik_ref[pl.ds(k0, tk), 0:IDX_DIM]
        acc = jnp.zeros((tq, tk), F32)
        for h in range(IDX_HEADS):
            d = _dot_nt(iq_ref[:, h * IDX_DIM:(h + 1) * IDX_DIM], ikb)
            acc = acc + jnp.maximum(d, 0.0) * sm_ref[:, IDX_DIM + h:IDX_DIM + h + 1]
        col = k0 + lax.broadcasted_iota(I32, (tq, tk), 1)
        key_sc[kb] = _float_key(jnp.where(col <= row, acc, -jnp.inf))
        return carry

    lax.fori_loop(0, nkb, score_block, 0)

    def count_ge(cand):
        def body(kb, part):
            ge = jnp.where(key_sc[kb] >= cand, 1.0, 0.0)
            for c in range(tk // LANE):
                part = part + ge[:, c * LANE:(c + 1) * LANE]
            return part
        part = lax.fori_loop(0, nkb, body, jnp.zeros((tq, LANE), F32))
        return jnp.sum(part, axis=-1, keepdims=True)

    kf = float(topk)
    t0 = jnp.where(count_ge(jnp.zeros((tq, 1), I32)) >= kf, 0, _INT_MIN).astype(I32)

    def bit_step(it, t):
        cand = t | lax.shift_left(jnp.int32(1), jnp.int32(30) - it)
        return jnp.where(count_ge(cand) >= kf, cand, t)

    thr = lax.fori_loop(0, 31, bit_step, t0)
    thr = jnp.maximum(thr, _KEY_NEG_INF + 1)

    m_sc[...] = jnp.full(m_sc.shape, _NEG_BIG, F32)
    l_sc[...] = jnp.zeros(l_sc.shape, F32)
    acc_sc[...] = jnp.zeros(acc_sc.shape, F32)

    def attend_block(kb, carry):
        k0 = pl.multiple_of(kb * tk, tk)
        sel = key_sc[kb] >= thr
        for h in range(ATT_HEADS):
            sl = slice(h * ATT_DIM, (h + 1) * ATT_DIM)
            s = _dot_nt(q_ref[:, sl], k_ref[pl.ds(k0, tk), sl]) * ATT_SCALE
            s = jnp.where(sel, s, _NEG_BIG)
            m_old = m_sc[h]
            m_new = jnp.maximum(m_old, jnp.max(s, axis=-1, keepdims=True))
            p = jnp.exp(s - m_new)
            a = jnp.exp(m_old - m_new)
            l_sc[h] = a * l_sc[h] + jnp.sum(p, axis=-1, keepdims=True)
            acc_sc[:, sl] = a * acc_sc[:, sl] + _dot(p.astype(v_ref.dtype), v_ref[pl.ds(k0, tk), sl])
            m_sc[h] = m_new
        return carry

    lax.fori_loop(0, nkb, attend_block, 0)
    for h in range(ATT_HEADS):
        sl = slice(h * ATT_DIM, (h + 1) * ATT_DIM)
        o_ref[:, sl] = (acc_sc[:, sl] / l_sc[h]).astype(o_ref.dtype)


def _dsa_prompt(iqb, ikb, sm, qb, kb, vb, nb, seq, *, tq_pref=256, out_dtype=BF16):
    topk = min(TOPK_MAX, seq // 4)
    tq = _tile(seq, tq_pref, LANE)
    tk = tq
    nq = seq // tq
    W = ATT_WIDTH
    qblk = lambda b, i: (b * nq + i, 0)
    seqblk = lambda b, i: (b, 0)
    return pl.pallas_call(
        functools.partial(_dsa_kernel, topk, tq, tk),
        out_shape=jax.ShapeDtypeStruct((nb * seq, W), out_dtype),
        grid=(nb, nq),
        in_specs=[pl.BlockSpec((tq, IDX_WIDTH), qblk), pl.BlockSpec((seq, LANE), seqblk),
                  pl.BlockSpec((tq, LANE), qblk), pl.BlockSpec((tq, W), qblk),
                  pl.BlockSpec((seq, W), seqblk), pl.BlockSpec((seq, W), seqblk)],
        out_specs=pl.BlockSpec((tq, W), qblk),
        scratch_shapes=[pltpu.VMEM((seq // tk, tq, tk), I32),
                        pltpu.VMEM((ATT_HEADS, tq, 1), F32), pltpu.VMEM((ATT_HEADS, tq, 1), F32),
                        pltpu.VMEM((tq, W), F32)],
        compiler_params=_params(("parallel", "arbitrary")),
        name="dsa_prompt",
    )(iqb, ikb, sm, qb, kb, vb)


def _ret_tables(chunk):
    lg = jnp.log(1.0 - 2.0 ** (-5.0 - jnp.arange(RET_HEADS, dtype=F32)))
    i = jnp.arange(chunk, dtype=F32)
    diff = i[:, None] - i[None, :]
    decay = jnp.where(diff >= 0, jnp.exp(lg[:, None, None] * jnp.maximum(diff, 0.0)), 0.0)
    xi = jnp.exp(lg[:, None] * (i + 1.0))[:, :, None]
    zeta = jnp.exp(lg[:, None] * (chunk - 1.0 - i))[:, :, None]
    g_c = jnp.exp(lg * chunk)[:, None, None]
    return decay, xi, zeta, g_c


def _ret_kernel(nchunk, chunk, q_ref, k_ref, v_ref, g_ref, cos_ref, sin_ref, dec_ref, xi_ref,
                zeta_ref, gc_ref, gn_ref, o_ref, st_ref):
    decay, xi, zeta, g_c = dec_ref[0], xi_ref[0], zeta_ref[0], gc_ref[0]
    gn = gn_ref[...]

    def step(c, R):
        r0 = pl.multiple_of(c * chunk, chunk)
        rows = pl.ds(r0, chunk)
        cos, sin = cos_ref[rows, :], sin_ref[rows, :]
        q = q_ref[rows, :]
        k = k_ref[rows, :]
        qr = q * cos + pltpu.roll(q, RET_DIM // 2, axis=1) * sin
        kr = (k * cos + pltpu.roll(k, RET_DIM // 2, axis=1) * sin) * (RET_DIM ** -0.5)
        qb, vb = qr.astype(BF16), v_ref[rows, :].astype(BF16)
        inner = _dot_nt(qb, kr.astype(BF16)) * decay
        o = _dot(inner.astype(BF16), vb) + _dot(qb, R.astype(BF16)) * xi
        kz = (kr * zeta).astype(BF16)
        R = R * g_c + lax.dot_general(kz, vb, (((0,), (0,)), ((), ())), preferred_element_type=F32)
        mu = jnp.mean(o, axis=-1, keepdims=True)
        d = o - mu
        var = jnp.mean(d * d, axis=-1, keepdims=True)
        on = d * lax.rsqrt(var + LN_EPS) * gn
        o_ref[rows, :] = (on * _silu(g_ref[rows, :])).astype(o_ref.dtype)
        return R

    st_ref[0, 0] = lax.fori_loop(0, nchunk, step, jnp.zeros((RET_DIM, RET_DIM), F32))


def _retention_prompt(z, cos, sin, gn, nb, seq, *, out_dtype=BF16):
    chunk = RET_CHUNK if seq % RET_CHUNK == 0 else seq
    nchunk = seq // chunk
    decay, xi, zeta, g_c = _ret_tables(chunk)
    D, H = RET_DIM, RET_HEADS
    col = lambda c: (lambda b, h: (b, c * H + h))
    tab = lambda b, h: (0, 0)
    per_h = lambda b, h: (h, 0, 0)
    return pl.pallas_call(
        functools.partial(_ret_kernel, nchunk, chunk),
        out_shape=(jax.ShapeDtypeStruct((nb * seq, RET_WIDTH), out_dtype),
                   jax.ShapeDtypeStruct((nb, H, D, D), F32)),
        grid=(nb, H),
        in_specs=[pl.BlockSpec((seq, D), col(0)), pl.BlockSpec((seq, D), col(1)),
                  pl.BlockSpec((seq, D), col(2)), pl.BlockSpec((seq, D), col(3)),
                  pl.BlockSpec((seq, D), tab), pl.BlockSpec((seq, D), tab),
                  pl.BlockSpec((1, chunk, chunk), per_h), pl.BlockSpec((1, chunk, 1), per_h),
                  pl.BlockSpec((1, chunk, 1), per_h), pl.BlockSpec((1, 1, 1), per_h),
                  pl.BlockSpec((1, D), lambda b, h: (0, h))],
        out_specs=(pl.BlockSpec((seq, D), lambda b, h: (b, h)),
                   pl.BlockSpec((1, 1, D, D), lambda b, h: (b, h, 0, 0))),
        compiler_params=_params(("parallel", "parallel")),
        name="retention_prompt",
    )(z, z, z, z, cos, sin, decay, xi, zeta, g_c, gn)


def _causal_conv3_zero_prefix(u, w):
    row = lax.broadcasted_iota(I32, u.shape, 0)
    u1 = jnp.where(row >= 1, pltpu.roll(u, 1, axis=0), 0.0)
    u2 = jnp.where(row >= 2, pltpu.roll(u, 2, axis=0), 0.0)
    return w[0:1, :] * u2 + w[1:2, :] * u1 + w[2:3, :] * u


def _conv_kernel(seq, cb_ref, cc_ref, ch_ref, w_ref, o_ref, st_ref):
    u = cc_ref[...] * ch_ref[...]
    o_ref[...] = (cb_ref[...] * _causal_conv3_zero_prefix(u, w_ref[...])).astype(o_ref.dtype)
    st_ref[0] = u[seq - (CONV_K - 1):, :]


def _conv_prompt(z, w, nb, seq, *, out_dtype=BF16):
    C = CONV_WIDTH
    c0 = _Z_CONV // C
    col = lambda c: (lambda b: (b, c0 + c))
    return pl.pallas_call(
        functools.partial(_conv_kernel, seq),
        out_shape=(jax.ShapeDtypeStruct((nb * seq, C), out_dtype),
                   jax.ShapeDtypeStruct((nb, CONV_K - 1, C), F32)),
        grid=(nb,),
        in_specs=[pl.BlockSpec((seq, C), col(0)), pl.BlockSpec((seq, C), col(1)),
                  pl.BlockSpec((seq, C), col(2)), pl.BlockSpec((CONV_K, C), lambda b: (0, 0))],
        out_specs=(pl.BlockSpec((seq, C), lambda b: (b, 0)),
                   pl.BlockSpec((1, CONV_K - 1, C), lambda b: (b, 0, 0))),
        compiler_params=_params(("parallel",)),
        name="conv_prompt",
    )(z, z, z, w)


def _ffn_gate_kernel(ha_ref, hb_ref, wa_ref, wb_ref, o_ref):
    a = _causal_conv3_zero_prefix(ha_ref[...], wa_ref[...])
    b = _causal_conv3_zero_prefix(hb_ref[...], wb_ref[...])
    o_ref[...] = (_silu(a) * b).astype(o_ref.dtype)


def _ffn_gate_prompt(h, w, nb, seq, *, tc_pref=256, out_dtype=BF16):
    F = h.shape[1] // 2
    tc = _tile(F, tc_pref, LANE)
    nc = F // tc
    return pl.pallas_call(
        _ffn_gate_kernel,
        out_shape=jax.ShapeDtypeStruct((nb * seq, F), out_dtype),
        grid=(nb, nc),
        in_specs=[pl.BlockSpec((seq, tc), lambda b, j: (b, j)),
                  pl.BlockSpec((seq, tc), lambda b, j: (b, nc + j)),
                  pl.BlockSpec((FFN_K, tc), lambda b, j: (0, j)),
                  pl.BlockSpec((FFN_K, tc), lambda b, j: (0, nc + j))],
        out_specs=pl.BlockSpec((seq, tc), lambda b, j: (b, j)),
        compiler_params=_params(("parallel", "parallel")),
        name="ffn_gate_prompt",
    )(h, h, w, w)


def _sample_mix_kernel(nb, z_ref, st_ref, cos_ref, sin_ref, gam_ref, gn_ref, p0_ref, p1_ref, cw_ref,
                       ro_ref, nst_ref, co_ref, u_ref):
    D = RET_DIM
    rowi = lax.broadcasted_iota(I32, (nb, D), 0)
    di = lax.broadcasted_iota(I32, (D, D), 0)
    dj = lax.broadcasted_iota(I32, (D, D), 1)
    cos, sin = cos_ref[...], sin_ref[...]
    for h in range(RET_HEADS):
        sl = lambda c: slice((c * RET_HEADS + h) * D, (c * RET_HEADS + h + 1) * D)
        q, k, v, g = z_ref[:, sl(0)], z_ref[:, sl(1)], z_ref[:, sl(2)], z_ref[:, sl(3)]
        gam = gam_ref[:, h * D:(h + 1) * D]
        qr = q * cos + pltpu.roll(q, D // 2, axis=1) * sin
        kr = (k * cos + pltpu.roll(k, D // 2, axis=1) * sin) * (D ** -0.5)
        qf, kf, vf = (t.astype(BF16).astype(F32) for t in (qr, kr, v))
        inner = jnp.sum(qf * kf, axis=-1, keepdims=True)
        o = inner.astype(BF16).astype(F32) * vf
        cross = jnp.zeros((nb, D), F32)
        for b in range(nb):
            R = st_ref[b, h]
            cross = cross + jnp.where(rowi == b, _dot(qf.astype(BF16), R.astype(BF16)), 0.0)
            kdiag = jnp.where(di == dj, jnp.broadcast_to(kf[b:b + 1, :], (D, D)), 0.0).astype(BF16)
            vrows = jnp.broadcast_to(vf[b:b + 1, :], (D, D)).astype(BF16)
            nst_ref[b, h] = R * gam + _dot(kdiag, vrows)
        o = o + cross * gam
        mu = jnp.mean(o, axis=-1, keepdims=True)
        d = o - mu
        var = jnp.mean(d * d, axis=-1, keepdims=True)
        on = d * lax.rsqrt(var + LN_EPS) * gn_ref[:, h * D:(h + 1) * D]
        ro_ref[:, h * D:(h + 1) * D] = on * _silu(g)
    C = CONV_WIDTH
    cb, cc, ch = (z_ref[:, _Z_CONV + c * C:_Z_CONV + (c + 1) * C] for c in range(3))
    u = cc * ch
    w = cw_ref[...]
    co_ref[...] = cb * (w[0:1, :] * p0_ref[...] + w[1:2, :] * p1_ref[...] + w[2:3, :] * u)
    u_ref[...] = u


def _sample_mix(z, state, cos, sin, gam, gn, p0, p1, cw):
    nb = z.shape[0]
    D, H, C = RET_DIM, RET_HEADS, CONV_WIDTH
    return pl.pallas_call(
        functools.partial(_sample_mix_kernel, nb),
        out_shape=(jax.ShapeDtypeStruct((nb, RET_WIDTH), F32), jax.ShapeDtypeStruct((nb, H, D, D), F32),
                   jax.ShapeDtypeStruct((nb, C), F32), jax.ShapeDtypeStruct((nb, C), F32)),
        name="sample_mix",
        compiler_params=pltpu.CompilerParams(vmem_limit_bytes=_VMEM_LIMIT),
    )(z, state, cos, sin, gam, gn, p0, p1, cw)


def _sample_gate_kernel(F, h_ref, p0_ref, p1_ref, w_ref, o_ref):
    w = w_ref[...]
    y = w[0:1, :] * p0_ref[...] + w[1:2, :] * p1_ref[...] + w[2:3, :] * h_ref[...]
    o_ref[...] = _silu(y[:, :F]) * y[:, F:]


def _sample_gate(h, p0, p1, w):
    nb, F2 = h.shape
    return pl.pallas_call(
        functools.partial(_sample_gate_kernel, F2 // 2),
        out_shape=jax.ShapeDtypeStruct((nb, F2 // 2), F32),
        name="sample_gate",
    )(h, p0, p1, w)


def _sample_score_kernel(layer, topk, n_pages, group, pt_ref, iq_ref, w_ref, ikn_ref, cache_ref,
                         mask_ref, new_ref, buf, sem, key_sc):
    b = pl.program_id(0)

    def page_copy(p):
        return pltpu.make_async_copy(cache_ref.at[layer, pt_ref[b, p]], buf.at[p], sem.at[0])

    def start(p, c):
        page_copy(p).start()
        return c

    def wait(p, c):
        page_copy(p).wait()
        return c

    lax.fori_loop(0, n_pages, start, 0)
    lax.fori_loop(0, n_pages, wait, 0)

    iq = iq_ref[0]
    w = w_ref[0]
    gk = group * PAGE_SIZE
    for g in range(n_pages // group):
        keys = buf[g * group:(g + 1) * group].reshape(gk, IDX_DIM).astype(iq.dtype)
        d = _dot_nt(iq, keys)
        s = jnp.sum(jnp.maximum(d, 0.0) * w, axis=0, keepdims=True)
        key_sc[:, g * gk:(g + 1) * gk] = _float_key(s)
    dn = jnp.sum(iq.astype(F32) * ikn_ref[0].astype(iq.dtype).astype(F32), axis=-1, keepdims=True)
    s_new = jnp.sum(jnp.maximum(dn, 0.0) * w, axis=0, keepdims=True)
    key_new = _float_key(s_new)

    def count_ge(cand):
        ge = jnp.where(key_sc[...] >= cand, 1.0, 0.0)
        return jnp.sum(ge, axis=-1, keepdims=True) + jnp.where(key_new >= cand, 1.0, 0.0)

    kf = float(topk)
    t0 = jnp.where(count_ge(jnp.zeros((1, 1), I32)) >= kf, 0, _INT_MIN).astype(I32)

    def bit_step(it, t):
        cand = t | lax.shift_left(jnp.int32(1), jnp.int32(30) - it)
        return jnp.where(count_ge(cand) >= kf, cand, t)

    thr = lax.fori_loop(0, 31, bit_step, t0)
    mask_ref[0] = jnp.where(key_sc[...] >= thr, 1, 0).astype(I32)
    new_ref[0] = jnp.broadcast_to(jnp.where(key_new >= thr, 1, 0).astype(I32), (1, LANE))


def _sample_score(page_table, iq, w, ikn, cache_idx_k, layer, topk):
    nb, n_pages = page_table.shape
    past = n_pages * PAGE_SIZE
    group = _tile(n_pages, 16, 1)
    grid_spec = pltpu.PrefetchScalarGridSpec(
        num_scalar_prefetch=1,
        grid=(nb,),
        in_specs=[pl.BlockSpec((1, IDX_HEADS, IDX_DIM), lambda b, pt: (b, 0, 0)),
                  pl.BlockSpec((1, IDX_HEADS, 1), lambda b, pt: (b, 0, 0)),
                  pl.BlockSpec((1, 1, IDX_DIM), lambda b, pt: (b, 0, 0)),
                  pl.BlockSpec(memory_space=pl.ANY)],
        out_specs=(pl.BlockSpec((1, 1, past), lambda b, pt: (b, 0, 0)),
                   pl.BlockSpec((1, 1, LANE), lambda b, pt: (b, 0, 0))),
        scratch_shapes=[pltpu.VMEM((n_pages, PAGE_SIZE, IDX_DIM), F32),
                        pltpu.SemaphoreType.DMA((1,)),
                        pltpu.VMEM((1, past), I32)],
    )
    return pl.pallas_call(
        functools.partial(_sample_score_kernel, layer, topk, n_pages, group),
        out_shape=(jax.ShapeDtypeStruct((nb, 1, past), I32), jax.ShapeDtypeStruct((nb, 1, LANE), I32)),
        grid_spec=grid_spec,
        compiler_params=_params(("arbitrary",)),
        name="sample_score",
    )(page_table, iq, w, ikn, cache_idx_k)


def _sample_attn_kernel(layer, topk, phys_ref, off_ref, q_ref, kn_ref, vn_ref, inpast_ref,
                        ck_ref, cv_ref, o_ref, kbuf, vbuf, sem):
    b = pl.program_id(0)

    def copies(j):
        src = (layer, phys_ref[b, j], off_ref[b, j])
        return (pltpu.make_async_copy(ck_ref.at[src], kbuf.at[j], sem.at[0]),
                pltpu.make_async_copy(cv_ref.at[src], vbuf.at[j], sem.at[1]))

    def start(j, c):
        ck, cv = copies(j)
        ck.start()
        cv.start()
        return c

    def wait(j, c):
        ck, cv = copies(j)
        ck.wait()
        cv.wait()
        return c

    lax.fori_loop(0, topk, start, 0)
    lax.fori_loop(0, topk, wait, 0)

    inpast = inpast_ref[0] > 0
    ks = jnp.where(inpast, kbuf[...], kn_ref[...])
    vs = jnp.where(inpast, vbuf[...], vn_ref[...])
    q = q_ref[...].astype(BF16).astype(F32)
    s = jnp.sum(q * ks.astype(BF16).astype(F32), axis=-1, keepdims=True) * ATT_SCALE
    m = jnp.max(s, axis=0, keepdims=True)
    e = jnp.exp(s - m)
    p = e / jnp.sum(e, axis=0, keepdims=True)
    o_ref[...] = jnp.sum(p.astype(BF16).astype(F32) * vs.astype(BF16).astype(F32), axis=0, keepdims=True)


def _sample_attn(phys, off, q, kn, vn, inpast, cache_k, cache_v, layer, topk):
    nb = q.shape[0]
    H, D = ATT_HEADS, ATT_DIM
    per_b = lambda b, ph, of: (b, 0, 0)
    grid_spec = pltpu.PrefetchScalarGridSpec(
        num_scalar_prefetch=2,
        grid=(nb,),
        in_specs=[pl.BlockSpec((1, H, D), per_b), pl.BlockSpec((1, H, D), per_b),
                  pl.BlockSpec((1, H, D), per_b),
                  pl.BlockSpec((1, topk, 1, 1), lambda b, ph, of: (b, 0, 0, 0)),
                  pl.BlockSpec(memory_space=pl.ANY), pl.BlockSpec(memory_space=pl.ANY)],
        out_specs=pl.BlockSpec((1, H, D), per_b),
        scratch_shapes=[pltpu.VMEM((topk, H, D), F32), pltpu.VMEM((topk, H, D), F32),
                        pltpu.SemaphoreType.DMA((2,))],
    )
    return pl.pallas_call(
        functools.partial(_sample_attn_kernel, layer, topk),
        out_shape=jax.ShapeDtypeStruct((nb, H, D), F32),
        grid_spec=grid_spec,
        compiler_params=_params(("arbitrary",)),
        name="sample_attn",
    )(phys, off, q, kn, vn, inpast, cache_k, cache_v)


def kernel(x_prompt, x_sample, cache_k, cache_v, cache_idx_k, state_ret, state_conv, state_ffn,
           page_table, w_in, ret_gn_g, idx_kn_g, idx_kn_b, conv_w, w_out, ln1_g, ln1_b,
           w_up, ffn_conv_w, w_down, ln2_g, ln2_b):
    B, S, D = x_prompt.shape
    DB, T, _ = x_sample.shape
    assert T == 1 and S % PAGE_SIZE == 0 and w_in.shape[-1] == _D_IN
    depth = w_in.shape[0]
    n_pages = page_table.shape[1]
    past = n_pages * PAGE_SIZE
    F2 = w_up.shape[-1]
    alpha = (2 * depth) ** 0.25
    topk_s = min(TOPK_MAX, (past + T) // 4)

    pos_p = jnp.arange(S, dtype=jnp.int32)
    pos_s = past + jnp.arange(T, dtype=jnp.int32)
    att_tabs_p = _rope_tables(pos_p, ROT_DIM, ROPE_THETA, ATT_DIM)
    idx_tabs_p = _rope_tables(pos_p, IDX_ROT_DIM, ROPE_THETA, IDX_DIM)
    rc, ru, rd = _rope_tables(pos_p, RET_DIM, RET_THETA, RET_DIM)
    ret_cos_p, ret_sin_p = rc, ru + rd
    bcast = lambda t: jnp.broadcast_to(t, (DB, LANE))
    att_tabs_s = tuple(bcast(t) for t in _rope_tables(pos_s, ROT_DIM, ROPE_THETA, ATT_DIM))
    idx_tabs_s = tuple(bcast(t) for t in _rope_tables(pos_s, IDX_ROT_DIM, ROPE_THETA, IDX_DIM))
    rc, ru, rd = _rope_tables(pos_s, RET_DIM, RET_THETA, RET_DIM)
    ret_cos_s, ret_sin_s = bcast(rc), bcast(ru + rd)
    gamma = 1.0 - 2.0 ** (-5.0 - jnp.arange(RET_HEADS, dtype=F32))
    gamma = jnp.exp(jnp.log(gamma))
    gam_row = jnp.repeat(gamma, RET_DIM)[None, :]

    xp = x_prompt.reshape(B * S, D)
    xp_lo = xp.astype(BF16)
    xs = x_sample.reshape(DB, D)

    outs_p = [[] for _ in range(6)]
    outs_s = [[] for _ in range(6)]
    row2 = lambda v: v.reshape(1, -1)
    pad_small = lambda v: jnp.pad(v, (0, _SMALL_W - v.shape[0])).reshape(1, _SMALL_W)

    for l in range(depth):
        wl = w_in[l]
        w_main = jnp.concatenate([wl[:, :_O_IK], wl[:, _O_CONV:]], axis=1).astype(BF16)
        w_small = jnp.pad(wl[:, _O_IK:_O_CONV], ((0, 0), (0, _SMALL_W - (_O_CONV - _O_IK)))).astype(BF16)
        w_out_l, w_up_l, w_down_l = w_out[l].astype(BF16), w_up[l].astype(BF16), w_down[l].astype(BF16)
        gn = row2(ret_gn_g[l])
        kg, kb_ = pad_small(idx_kn_g[l]), pad_small(idx_kn_b[l])
        g1, b1, g2, b2 = row2(ln1_g[l]), row2(ln1_b[l]), row2(ln2_g[l]), row2(ln2_b[l])

        z, zs = _inproj(xp_lo, w_main, w_small)
        ro, p_ret = _retention_prompt(z, ret_cos_p, ret_sin_p, gn, B, S)
        qb, kb, vb, kf, iqb, ikb, sm = _prep(z, zs, att_tabs_p, idx_tabs_p, kg, kb_, S)
        ao = _dsa_prompt(iqb, ikb, sm, qb, kb, vb, B, S)
        co, p_conv = _conv_prompt(z, conv_w[l], B, S)
        x1, x1_lo = _outproj_ln(ro, ao, co, w_out_l, xp, g1, b1, alpha)
        h = _matmul(x1_lo, w_up_l, name="up")
        gt = _ffn_gate_prompt(h, ffn_conv_w[l], B, S)
        xp, xp_lo = _down_ln(gt, w_down_l, x1, g2, b2, alpha)
        outs_p[0].append(kf.reshape(B, S, ATT_HEADS, ATT_DIM))
        outs_p[1].append(z[:, _O_ATT + 2 * ATT_WIDTH:_O_ATT + 3 * ATT_WIDTH].reshape(B, S, ATT_HEADS, ATT_DIM))
        outs_p[2].append(sm[:, :IDX_DIM].reshape(B, S, IDX_DIM))
        outs_p[3].append(p_ret)
        outs_p[4].append(p_conv)
        outs_p[5].append(h.reshape(B, S, F2)[:, S - (FFN_K - 1):, :])

        z, zs = _inproj(xs, w_main, w_small)
        ro, s_ret, co, u_new = _sample_mix(z, state_ret[l], ret_cos_s, ret_sin_s, gam_row, gn,
                                           state_conv[l, :, 0], state_conv[l, :, 1], conv_w[l])
        q_s, _, _, k_s, iq_s, _, sm_s = _prep(z, zs, att_tabs_s, idx_tabs_s, kg, kb_, DB, lowp_dtype=F32)
        v_s = z[:, _O_ATT + 2 * ATT_WIDTH:_O_ATT + 3 * ATT_WIDTH]
        ik_s = sm_s[:, :IDX_DIM]
        iw_s = sm_s[:, IDX_DIM:IDX_DIM + IDX_HEADS]
        mask, new_sel = _sample_score(page_table, iq_s.reshape(DB, IDX_HEADS, IDX_DIM).astype(BF16),
                                      iw_s.reshape(DB, IDX_HEADS, 1), ik_s.reshape(DB, 1, IDX_DIM),
                                      cache_idx_k, l, topk_s)
        full = jnp.concatenate([mask[:, 0, :], new_sel[:, 0, :1]], axis=1)
        cand = jnp.where(full > 0, jnp.arange(past + 1, dtype=jnp.int32)[None, :], past + 1)
        sel = jnp.sort(cand, axis=1)[:, :topk_s]
        inpast = sel < past
        sp = jnp.minimum(sel, past - 1)
        phys = jnp.take_along_axis(page_table, sp // PAGE_SIZE, axis=1)
        off = sp % PAGE_SIZE
        hd = lambda t: t.reshape(DB, ATT_HEADS, ATT_DIM)
        ao = _sample_attn(phys, off, hd(q_s), hd(k_s), hd(v_s),
                          inpast.astype(jnp.int32).reshape(DB, topk_s, 1, 1), cache_k, cache_v, l, topk_s)
        x1, _ = _outproj_ln(ro, ao.reshape(DB, ATT_WIDTH), co, w_out_l, xs, g1, b1, alpha, lowp_dtype=F32)
        h = _matmul(x1, w_up_l, name="up_s")
        gt = _sample_gate(h, state_ffn[l, :, 0], state_ffn[l, :, 1], ffn_conv_w[l])
        xs_new, _ = _down_ln(gt, w_down_l, x1, g2, b2, alpha, lowp_dtype=F32)
        outs_s[0].append(k_s.reshape(DB, T, ATT_HEADS, ATT_DIM))
        outs_s[1].append(v_s.reshape(DB, T, ATT_HEADS, ATT_DIM))
        outs_s[2].append(ik_s.reshape(DB, T, IDX_DIM))
        outs_s[3].append(s_ret)
        outs_s[4].append(jnp.stack([state_conv[l, :, 1], u_new], axis=1))
        outs_s[5].append(jnp.stack([state_ffn[l, :, 1], h], axis=1))
        xs = xs_new

    n_pp = S // PAGE_SIZE
    st = lambda xs_: jnp.stack(xs_)
    return (xp.reshape(B, S, D), xs.reshape(DB, T, D),
            st(outs_p[0]).reshape(depth, B, n_pp, PAGE_SIZE, ATT_HEADS, ATT_DIM),
            st(outs_p[1]).reshape(depth, B, n_pp, PAGE_SIZE, ATT_HEADS, ATT_DIM),
            st(outs_p[2]).reshape(depth, B, n_pp, PAGE_SIZE, IDX_DIM),
            st(outs_p[3]), st(outs_p[4]), st(outs_p[5]),
            st(outs_s[0]), st(outs_s[1]), st(outs_s[2]), st(outs_s[3]), st(outs_s[4]), st(outs_s[5]))
```

```python
import functools

import numpy as np
import jax
import jax.numpy as jnp
from jax import lax
from jax.experimental import pallas as pl
from jax.experimental.pallas import tpu as pltpu

F32 = jnp.float32
BF16 = jnp.bfloat16
I32 = jnp.int32

PAGE_SIZE = 128
RET_HEADS, RET_DIM, RET_CHUNK, RET_THETA = 4, 128, 128, 10000.0
RET_WIDTH = RET_HEADS * RET_DIM
ATT_HEADS, ATT_DIM = 8, 128
ATT_WIDTH = ATT_HEADS * ATT_DIM
ROT_DIM = ATT_DIM // 4
ROPE_THETA = 500000.0
IDX_HEADS, IDX_DIM = 16, 64
IDX_WIDTH = IDX_HEADS * IDX_DIM
IDX_ROT_DIM = IDX_DIM // 4
TOPK_MAX = 256
ATT_SCALE = ATT_DIM ** -0.5
IDX_W_SCALE = (IDX_HEADS * IDX_DIM) ** -0.5
CONV_WIDTH, CONV_K = 512, 3
FFN_K = 3
LN_EPS = 1e-5

_O_RET = 0
_O_ATT = 4 * RET_WIDTH
_O_IQ = _O_ATT + 3 * ATT_WIDTH
_O_IK = _O_IQ + IDX_WIDTH
_O_IW = _O_IK + IDX_DIM
_O_CONV = _O_IW + IDX_HEADS
_D_IN = _O_CONV + 3 * CONV_WIDTH
_Z_CONV = _O_IK
_Z_MAIN = _Z_CONV + 3 * CONV_WIDTH
LANE = 128
_SMALL_W = LANE

_NEG_BIG = -1e30
_INT_MIN = -2 ** 31
_KEY_NEG_INF = int(np.int32(np.uint32(0xFF800000 ^ 0x7FFFFFFF)))

_VMEM_LIMIT = 56 * 1024 * 1024


def _params(sem):
    return pltpu.CompilerParams(dimension_semantics=sem, vmem_limit_bytes=_VMEM_LIMIT)


def _tile(dim, pref, mult):
    t = min(pref, dim)
    t -= t % mult
    while t >= mult:
        if dim % t == 0:
            return t
        t -= mult
    return dim


def _float_key(s):
    b = lax.bitcast_convert_type(s, I32)
    return b ^ ((b >> 31) & jnp.int32(0x7FFFFFFF))


def _layer_norm_rows(r, g, b):
    mu = jnp.mean(r, axis=-1, keepdims=True)
    d = r - mu
    var = jnp.mean(d * d, axis=-1, keepdims=True)
    return d * lax.rsqrt(var + LN_EPS) * g + b


def _silu(x):
    return x * (1.0 / (1.0 + jnp.exp(-x)))


def _dot(a, b):
    return jnp.dot(a, b, preferred_element_type=F32)


def _dot_nt(a, b):
    return lax.dot_general(a, b, (((1,), (1,)), ((), ())), preferred_element_type=F32)


def _mm_kernel(x_ref, w_ref, o_ref):
    o_ref[...] = _dot(x_ref[...].astype(BF16), w_ref[...]).astype(o_ref.dtype)


def _matmul(x, w, *, tm_pref=1024, tn_pref=512, out_dtype=F32, name="mm"):
    M, K = x.shape
    N = w.shape[1]
    tm = _tile(M, tm_pref, 8)
    tn = _tile(N, tn_pref, LANE)
    return pl.pallas_call(
        _mm_kernel,
        out_shape=jax.ShapeDtypeStruct((M, N), out_dtype),
        grid=(M // tm, N // tn),
        in_specs=[pl.BlockSpec((tm, K), lambda i, j: (i, 0)),
                  pl.BlockSpec((K, tn), lambda i, j: (0, j))],
        out_specs=pl.BlockSpec((tm, tn), lambda i, j: (i, j)),
        compiler_params=_params(("parallel", "arbitrary")),
        name=name,
    )(x, w)


def _inproj_kernel(x_ref, w_ref, ws_ref, o_ref, os_ref):
    x = x_ref[...].astype(BF16)
    o_ref[...] = _dot(x, w_ref[...])

    @pl.when(pl.program_id(1) == 0)
    def _():
        os_ref[...] = _dot(x, ws_ref[...])


def _inproj(x, w_main, w_small, *, tm_pref=1024, tn_pref=512):
    M, K = x.shape
    N = w_main.shape[1]
    tm = _tile(M, tm_pref, 8)
    tn = _tile(N, tn_pref, LANE)
    return pl.pallas_call(
        _inproj_kernel,
        out_shape=(jax.ShapeDtypeStruct((M, N), F32), jax.ShapeDtypeStruct((M, _SMALL_W), F32)),
        grid=(M // tm, N // tn),
        in_specs=[pl.BlockSpec((tm, K), lambda i, j: (i, 0)),
                  pl.BlockSpec((K, tn), lambda i, j: (0, j)),
                  pl.BlockSpec((K, _SMALL_W), lambda i, j: (0, 0))],
        out_specs=(pl.BlockSpec((tm, tn), lambda i, j: (i, j)),
                   pl.BlockSpec((tm, _SMALL_W), lambda i, j: (i, 0))),
        compiler_params=_params(("parallel", "arbitrary")),
        name="inproj",
    )(x, w_main, w_small)


def _outproj_ln_kernel(alpha, ro_ref, ao_ref, co_ref, w_ref, x_ref, g_ref, b_ref, o_ref, ob_ref):
    y = _dot(ro_ref[...].astype(BF16), w_ref[0:RET_WIDTH, :])
    y += _dot(ao_ref[...].astype(BF16), w_ref[RET_WIDTH:RET_WIDTH + ATT_WIDTH, :])
    y += _dot(co_ref[...].astype(BF16), w_ref[RET_WIDTH + ATT_WIDTH:, :])
    out = _layer_norm_rows(alpha * x_ref[...] + y, g_ref[...], b_ref[...])
    o_ref[...] = out
    ob_ref[...] = out.astype(ob_ref.dtype)


def _outproj_ln(ro, ao, co, w_out, x, g, b, alpha, *, tm_pref=512, lowp_dtype=BF16):
    M, D = x.shape
    tm = _tile(M, tm_pref, 8)
    row = lambda i: (i, 0)
    fixed = lambda i: (0, 0)
    return pl.pallas_call(
        functools.partial(_outproj_ln_kernel, alpha),
        out_shape=(jax.ShapeDtypeStruct((M, D), F32), jax.ShapeDtypeStruct((M, D), lowp_dtype)),
        grid=(M // tm,),
        in_specs=[pl.BlockSpec((tm, RET_WIDTH), row), pl.BlockSpec((tm, ATT_WIDTH), row),
                  pl.BlockSpec((tm, CONV_WIDTH), row), pl.BlockSpec(w_out.shape, fixed),
                  pl.BlockSpec((tm, D), row), pl.BlockSpec((1, D), fixed), pl.BlockSpec((1, D), fixed)],
        out_specs=(pl.BlockSpec((tm, D), row), pl.BlockSpec((tm, D), row)),
        compiler_params=_params(("parallel",)),
        name="outproj_ln",
    )(ro, ao, co, w_out, x, g, b)


def _down_ln_kernel(alpha, nk, gt_ref, w_ref, x_ref, g_ref, b_ref, o_ref, ob_ref, acc_ref):
    k = pl.program_id(1)

    @pl.when(k == 0)
    def _():
        acc_ref[...] = jnp.zeros_like(acc_ref)

    acc_ref[...] += _dot(gt_ref[...].astype(BF16), w_ref[...])

    @pl.when(k == nk - 1)
    def _():
        out = _layer_norm_rows(alpha * x_ref[...] + acc_ref[...], g_ref[...], b_ref[...])
        o_ref[...] = out
        ob_ref[...] = out.astype(ob_ref.dtype)


def _down_ln(gt, w_down, x, g, b, alpha, *, tm_pref=512, tk_pref=512, lowp_dtype=BF16):
    M, D = x.shape
    K = gt.shape[1]
    tm = _tile(M, tm_pref, 8)
    tk = _tile(K, tk_pref, LANE)
    nk = K // tk
    return pl.pallas_call(
        functools.partial(_down_ln_kernel, alpha, nk),
        out_shape=(jax.ShapeDtypeStruct((M, D), F32), jax.ShapeDtypeStruct((M, D), lowp_dtype)),
        grid=(M // tm, nk),
        in_specs=[pl.BlockSpec((tm, tk), lambda i, k: (i, k)),
                  pl.BlockSpec((tk, D), lambda i, k: (k, 0)),
                  pl.BlockSpec((tm, D), lambda i, k: (i, 0)),
                  pl.BlockSpec((1, D), lambda i, k: (0, 0)),
                  pl.BlockSpec((1, D), lambda i, k: (0, 0))],
        out_specs=(pl.BlockSpec((tm, D), lambda i, k: (i, 0)),
                   pl.BlockSpec((tm, D), lambda i, k: (i, 0))),
        scratch_shapes=[pltpu.VMEM((tm, D), F32)],
        compiler_params=_params(("parallel", "arbitrary")),
        name="down_ln",
    )(gt, w_down, x, g, b)


def _rope_tables(pos, rot_dim, theta, period):
    half = rot_dim // 2
    inv = theta ** (-jnp.arange(half, dtype=F32) / half)
    ang = pos.astype(F32)[:, None] * inv[None, :]
    cos, sin = jnp.cos(ang), jnp.sin(ang)
    L = pos.shape[0]
    rest = period - rot_dim
    c = jnp.concatenate([cos, cos, jnp.ones((L, rest), F32)], axis=1)
    s_up = jnp.concatenate([-sin, jnp.zeros((L, half + rest), F32)], axis=1)
    s_dn = jnp.concatenate([jnp.zeros((L, half), F32), sin, jnp.zeros((L, rest), F32)], axis=1)
    rep = LANE // period
    return tuple(jnp.tile(t, (1, rep)) for t in (c, s_up, s_dn))


def _rope128(x, c, s_up, s_dn, half):
    if 2 * half == LANE:
        return x * c + pltpu.roll(x, half, axis=1) * (s_up + s_dn)
    return x * c + pltpu.roll(x, LANE - half, axis=1) * s_up + pltpu.roll(x, half, axis=1) * s_dn


def _prep_kernel(transposed, aq_ref, ak_ref, av_ref, iq_ref, sm_ref, ac_ref, au_ref, ad_ref,
                 ic_ref, iu_ref, id_ref, kg_ref, kb_ref,
                 q_o, k_o, v_o, kf_o, iq_o, ik_o, sm_o):
    ac, au, ad = ac_ref[...], au_ref[...], ad_ref[...]
    ic, iu, idn = ic_ref[...], iu_ref[...], id_ref[...]
    for h in range(ATT_HEADS):
        sl = slice(h * ATT_DIM, (h + 1) * ATT_DIM)
        q_o[:, sl] = _rope128(aq_ref[:, sl], ac, au, ad, ROT_DIM // 2).astype(q_o.dtype)
        kr = _rope128(ak_ref[:, sl], ac, au, ad, ROT_DIM // 2)
        kf_o[:, sl] = kr
        k_o[:, sl] = kr.astype(k_o.dtype)
        if transposed:
            v_o[sl, :] = av_ref[:, sl].T.astype(v_o.dtype)
    if not transposed:
        v_o[...] = av_ref[...].astype(v_o.dtype)
    for c in range(IDX_WIDTH // LANE):
        sl = slice(c * LANE, (c + 1) * LANE)
        iq_o[:, sl] = _rope128(iq_ref[:, sl], ic, iu, idn, IDX_ROT_DIM // 2).astype(iq_o.dtype)
    sm = sm_ref[...]
    lane = lax.broadcasted_iota(I32, sm.shape, 1)
    is_k = lane < IDX_DIM
    mu = jnp.sum(jnp.where(is_k, sm, 0.0), axis=-1, keepdims=True) * (1.0 / IDX_DIM)
    d = jnp.where(is_k, sm - mu, 0.0)
    var = jnp.sum(d * d, axis=-1, keepdims=True) * (1.0 / IDX_DIM)
    n = d * lax.rsqrt(var + LN_EPS) * kg_ref[...] + kb_ref[...]
    roped = _rope128(n, ic, iu, idn, IDX_ROT_DIM // 2)
    out = jnp.where(is_k, roped, jnp.where(lane < IDX_DIM + IDX_HEADS, sm * IDX_W_SCALE, 0.0))
    sm_o[...] = out.T if transposed else out
    ik_o[...] = out.astype(ik_o.dtype)


def _prep(z, zs, att_tabs, idx_tabs, kg, kb, seq, *, transposed, tm_pref=256, lowp_dtype=BF16):
    M = z.shape[0]
    tm = _tile(seq, tm_pref, 8)
    nt = seq // tm
    col = lambda c: (lambda i: (i, c))
    tab = lambda i: (i % nt, 0)
    fixed = lambda i: (0, 0)
    W = ATT_WIDTH
    big = lambda dt: jax.ShapeDtypeStruct((M, W), dt)
    rows, small = pl.BlockSpec((tm, W), col(0)), pl.BlockSpec((tm, LANE), col(0))
    if transposed:
        v_shape, v_spec = jax.ShapeDtypeStruct((W, M), lowp_dtype), pl.BlockSpec((W, tm), lambda i: (0, i))
        sm_shape, sm_spec = jax.ShapeDtypeStruct((LANE, M), F32), pl.BlockSpec((LANE, tm), lambda i: (0, i))
    else:
        v_shape, v_spec = big(lowp_dtype), rows
        sm_shape, sm_spec = jax.ShapeDtypeStruct((M, LANE), F32), small
    return pl.pallas_call(
        functools.partial(_prep_kernel, transposed),
        out_shape=(big(lowp_dtype), big(lowp_dtype), v_shape, big(F32), big(lowp_dtype),
                   jax.ShapeDtypeStruct((M, LANE), lowp_dtype), sm_shape),
        grid=(M // tm,),
        in_specs=[pl.BlockSpec((tm, W), col(_O_ATT // W)), pl.BlockSpec((tm, W), col(_O_ATT // W + 1)),
                  pl.BlockSpec((tm, W), col(_O_ATT // W + 2)), pl.BlockSpec((tm, W), col(_O_IQ // W)),
                  pl.BlockSpec((tm, LANE), col(0))]
        + [pl.BlockSpec((tm, LANE), tab)] * 6
        + [pl.BlockSpec((1, LANE), fixed)] * 2,
        out_specs=(rows, rows, v_spec, rows, rows, small, sm_spec),
        compiler_params=_params(("parallel",)),
        name="prep",
    )(z, z, z, z, zs, *att_tabs, *idx_tabs, kg, kb)


def _fold8(x, op):
    return op(x.reshape(x.shape[0] // 8, 8, x.shape[1]), axis=0)


def _dsa_kernel(topk, tq, iq_ref, ik_ref, smT_ref, q_ref, k_ref, vT_ref, o_ref,
                key_sc, bias_sc, s_sc, acc_sc):
    tk = tq
    i = pl.program_id(1)
    nkb = i + 1
    qpos = i * tq + lax.broadcasted_iota(I32, (tk, tq), 1)

    def score_block(kb, carry):
        k0 = pl.multiple_of(kb * tk, tk)
        ikb = ik_ref[pl.ds(k0, tk), 0:IDX_DIM]
        acc = jnp.zeros((tk, tq), F32)
        for h in range(IDX_HEADS):
            d = _dot_nt(ikb, iq_ref[:, h * IDX_DIM:(h + 1) * IDX_DIM])
            acc = acc + jnp.maximum(d, 0.0) * smT_ref[IDX_DIM + h:IDX_DIM + h + 1, :]
        kpos = k0 + lax.broadcasted_iota(I32, (tk, tq), 0)
        key_sc[kb] = _float_key(jnp.where(kpos <= qpos, acc, -jnp.inf))
        return carry

    lax.fori_loop(0, nkb, score_block, 0)

    def count_ge(cand):
        def body(kb, part):
            return part + _fold8(jnp.where(key_sc[kb] >= cand, 1.0, 0.0), jnp.sum)
        part = lax.fori_loop(0, nkb, body, jnp.zeros((8, tq), F32))
        return jnp.sum(part, axis=0, keepdims=True)

    kf = float(topk)
    t0 = jnp.where(count_ge(jnp.zeros((1, tq), I32)) >= kf, 0, _INT_MIN).astype(I32)

    def bit_step(it, t):
        cand = t | lax.shift_left(jnp.int32(1), jnp.int32(30) - it)
        return jnp.where(count_ge(cand) >= kf, cand, t)

    thr = lax.fori_loop(0, 31, bit_step, t0)
    thr = jnp.maximum(thr, _KEY_NEG_INF + 1)
    has_ties = jnp.max(count_ge(thr)) > kf

    @pl.when(jnp.logical_not(has_ties))
    def _():
        def bias_block(kb, carry):
            bias_sc[kb] = jnp.where(key_sc[kb] >= thr, 0.0, _NEG_BIG)
            return carry

        lax.fori_loop(0, nkb, bias_block, 0)

    @pl.when(has_ties)
    def _():
        need = kf - count_ge(thr + 1)
        ki = lax.broadcasted_iota(I32, (tk, tk), 0)
        kj = lax.broadcasted_iota(I32, (tk, tk), 1)
        lower_strict = jnp.where(kj < ki, 1.0, 0.0).astype(BF16)

        def bias_block(kb, before):
            keys = key_sc[kb]
            eq = jnp.where(keys == thr, 1.0, 0.0)
            rank = _dot(lower_strict, eq.astype(BF16)) + before
            take = jnp.where(keys > thr, 1.0, eq * jnp.where(rank < need, 1.0, 0.0))
            bias_sc[kb] = jnp.where(take > 0.0, 0.0, _NEG_BIG)
            return before + jnp.sum(_fold8(eq, jnp.sum), axis=0, keepdims=True)

        lax.fori_loop(0, nkb, bias_block, jnp.zeros((1, tq), F32))

    heads = [slice(h * ATT_DIM, (h + 1) * ATT_DIM) for h in range(ATT_HEADS)]

    def scores_pass(kb, mparts):
        k0 = pl.multiple_of(kb * tk, tk)
        bias = bias_sc[kb]
        new = []
        for h, sl in enumerate(heads):
            s = _dot_nt(k_ref[pl.ds(k0, tk), sl], q_ref[:, sl]) + bias
            s_sc[h, kb] = s
            new.append(jnp.maximum(mparts[h], _fold8(s, jnp.max)))
        return tuple(new)

    mparts = lax.fori_loop(0, nkb, scores_pass,
                           tuple(jnp.full((8, tq), _NEG_BIG, F32) for _ in heads))
    ms = [jnp.max(mp, axis=0, keepdims=True) for mp in mparts]
    acc_sc[...] = jnp.zeros(acc_sc.shape, F32)
    c = ATT_SCALE * float(np.log2(np.e))

    def values_pass(kb, lparts):
        k0 = pl.multiple_of(kb * tk, tk)
        new = []
        for h, sl in enumerate(heads):
            p = jnp.exp2((s_sc[h, kb] - ms[h]) * c)
            new.append(lparts[h] + _fold8(p, jnp.sum))
            acc_sc[sl, :] += _dot(vT_ref[sl, pl.ds(k0, tk)], p.astype(vT_ref.dtype))
        return tuple(new)

    lparts = lax.fori_loop(0, nkb, values_pass, tuple(jnp.zeros((8, tq), F32) for _ in heads))
    for h, sl in enumerate(heads):
        l = jnp.sum(lparts[h], axis=0, keepdims=True)
        o_ref[:, sl] = (acc_sc[sl, :] / l).T.astype(o_ref.dtype)


def _dsa_prompt(iqb, ikb, smT, qb, kb, vT, nb, seq, *, tq_pref=256, out_dtype=BF16):
    topk = min(TOPK_MAX, seq // 4)
    tq = _tile(seq, tq_pref, LANE)
    nq = seq // tq
    W = ATT_WIDTH
    qblk = lambda b, i: (b * nq + i, 0)
    seqblk = lambda b, i: (b, 0)
    return pl.pallas_call(
        functools.partial(_dsa_kernel, topk, tq),
        out_shape=jax.ShapeDtypeStruct((nb * seq, W), out_dtype),
        grid=(nb, nq),
        in_specs=[pl.BlockSpec((tq, IDX_WIDTH), qblk), pl.BlockSpec((seq, LANE), seqblk),
                  pl.BlockSpec((LANE, tq), lambda b, i: (0, b * nq + i)), pl.BlockSpec((tq, W), qblk),
                  pl.BlockSpec((seq, W), seqblk), pl.BlockSpec((W, seq), lambda b, i: (0, b))],
        out_specs=pl.BlockSpec((tq, W), qblk),
        scratch_shapes=[pltpu.VMEM((nq, tq, tq), I32), pltpu.VMEM((nq, tq, tq), F32),
                        pltpu.VMEM((ATT_HEADS, nq, tq, tq), F32), pltpu.VMEM((W, tq), F32)],
        compiler_params=_params(("parallel", "arbitrary")),
        name="dsa_prompt",
    )(iqb, ikb, smT, qb, kb, vT)


def _ret_tables(chunk):
    lg = jnp.log(1.0 - 2.0 ** (-5.0 - jnp.arange(RET_HEADS, dtype=F32)))
    i = jnp.arange(chunk, dtype=F32)
    diff = i[:, None] - i[None, :]
    decay = jnp.where(diff >= 0, jnp.exp(lg[:, None, None] * jnp.maximum(diff, 0.0)), 0.0)
    xi = jnp.exp(lg[:, None] * (i + 1.0))[:, :, None]
    zeta = jnp.exp(lg[:, None] * (chunk - 1.0 - i))[:, :, None]
    g_c = jnp.exp(lg * chunk)[:, None, None]
    return decay, xi, zeta, g_c


def _ret_kernel(nchunk, chunk, q_ref, k_ref, v_ref, g_ref, cos_ref, sin_ref, dec_ref, xi_ref,
                zeta_ref, gc_ref, gn_ref, o_ref, st_ref):
    D = RET_DIM
    st_ref[...] = jnp.zeros(st_ref.shape, F32)

    def step(c, carry):
        r0 = pl.multiple_of(c * chunk, chunk)
        rows = pl.ds(r0, chunk)
        cos, sin = cos_ref[rows, :], sin_ref[rows, :]
        for h in range(RET_HEADS):
            sl = slice(h * D, (h + 1) * D)
            q = q_ref[rows, sl]
            k = k_ref[rows, sl]
            qr = q * cos + pltpu.roll(q, D // 2, axis=1) * sin
            kr = (k * cos + pltpu.roll(k, D // 2, axis=1) * sin) * (D ** -0.5)
            qb, vb = qr.astype(BF16), v_ref[rows, sl].astype(BF16)
            R = st_ref[0, h]
            inner = _dot_nt(qb, kr.astype(BF16)) * dec_ref[h]
            o = _dot(inner.astype(BF16), vb) + _dot(qb, R.astype(BF16)) * xi_ref[h]
            kz = (kr * zeta_ref[h]).astype(BF16)
            st_ref[0, h] = R * gc_ref[h] + lax.dot_general(kz, vb, (((0,), (0,)), ((), ())),
                                                           preferred_element_type=F32)
            mu = jnp.mean(o, axis=-1, keepdims=True)
            d = o - mu
            var = jnp.mean(d * d, axis=-1, keepdims=True)
            on = d * lax.rsqrt(var + LN_EPS) * gn_ref[:, sl]
            o_ref[rows, sl] = (on * _silu(g_ref[rows, sl])).astype(o_ref.dtype)
        return carry

    lax.fori_loop(0, nchunk, step, 0)


def _retention_prompt(z, cos, sin, gn, nb, seq, *, out_dtype=BF16):
    chunk = RET_CHUNK if seq % RET_CHUNK == 0 else seq
    nchunk = seq // chunk
    decay, xi, zeta, g_c = _ret_tables(chunk)
    D, H, W = RET_DIM, RET_HEADS, RET_WIDTH
    col = lambda c: (lambda b: (b, c))
    tab = lambda b: (0, 0)
    whole = lambda b: (0, 0, 0)
    return pl.pallas_call(
        functools.partial(_ret_kernel, nchunk, chunk),
        out_shape=(jax.ShapeDtypeStruct((nb * seq, W), out_dtype),
                   jax.ShapeDtypeStruct((nb, H, D, D), F32)),
        grid=(nb,),
        in_specs=[pl.BlockSpec((seq, W), col(0)), pl.BlockSpec((seq, W), col(1)),
                  pl.BlockSpec((seq, W), col(2)), pl.BlockSpec((seq, W), col(3)),
                  pl.BlockSpec((seq, D), tab), pl.BlockSpec((seq, D), tab),
                  pl.BlockSpec((H, chunk, chunk), whole), pl.BlockSpec((H, chunk, 1), whole),
                  pl.BlockSpec((H, chunk, 1), whole), pl.BlockSpec((H, 1, 1), whole),
                  pl.BlockSpec((1, W), tab)],
        out_specs=(pl.BlockSpec((seq, W), lambda b: (b, 0)),
                   pl.BlockSpec((1, H, D, D), lambda b: (b, 0, 0, 0))),
        compiler_params=_params(("parallel",)),
        name="retention_prompt",
    )(z, z, z, z, cos, sin, decay, xi, zeta, g_c, gn)


def _causal_conv3_zero_prefix(u, w):
    row = lax.broadcasted_iota(I32, u.shape, 0)
    u1 = jnp.where(row >= 1, pltpu.roll(u, 1, axis=0), 0.0)
    u2 = jnp.where(row >= 2, pltpu.roll(u, 2, axis=0), 0.0)
    return w[0:1, :] * u2 + w[1:2, :] * u1 + w[2:3, :] * u


def _conv_kernel(seq, cb_ref, cc_ref, ch_ref, w_ref, o_ref, st_ref):
    u = cc_ref[...] * ch_ref[...]
    o_ref[...] = (cb_ref[...] * _causal_conv3_zero_prefix(u, w_ref[...])).astype(o_ref.dtype)
    st_ref[0] = u[seq - (CONV_K - 1):, :]


def _conv_prompt(z, w, nb, seq, *, out_dtype=BF16):
    C = CONV_WIDTH
    c0 = _Z_CONV // C
    col = lambda c: (lambda b: (b, c0 + c))
    return pl.pallas_call(
        functools.partial(_conv_kernel, seq),
        out_shape=(jax.ShapeDtypeStruct((nb * seq, C), out_dtype),
                   jax.ShapeDtypeStruct((nb, CONV_K - 1, C), F32)),
        grid=(nb,),
        in_specs=[pl.BlockSpec((seq, C), col(0)), pl.BlockSpec((seq, C), col(1)),
                  pl.BlockSpec((seq, C), col(2)), pl.BlockSpec((CONV_K, C), lambda b: (0, 0))],
        out_specs=(pl.BlockSpec((seq, C), lambda b: (b, 0)),
                   pl.BlockSpec((1, CONV_K - 1, C), lambda b: (b, 0, 0))),
        compiler_params=_params(("parallel",)),
        name="conv_prompt",
    )(z, z, z, w)


_HALO = 16


def _ffn_kernel(alpha, nf, tiles_per_seq, x_ref, halo_ref, wa_ref, wb_ref, ca_ref, cb_ref, wd_ref,
                res_ref, g_ref, b_ref, o_ref, ob_ref, st_ref, acc_ref):
    i, j = pl.program_id(0), pl.program_id(1)
    tm, tf = x_ref.shape[0], wa_ref.shape[1]
    x, xh = x_ref[...], halo_ref[...]
    seq_start = (i % tiles_per_seq) == 0
    row = lax.broadcasted_iota(I32, (tm, tf), 0)

    def conv_branch(part, w_ref, taps_ref):
        h = _dot(x, w_ref[...])
        hh = _dot(xh, w_ref[...])
        p1 = jnp.where(seq_start, 0.0, hh[_HALO - 1:_HALO, :])
        p2 = jnp.where(seq_start, 0.0, hh[_HALO - 2:_HALO - 1, :])
        u1 = jnp.where(row >= 1, pltpu.roll(h, 1, axis=0), p1)
        u2 = jnp.where(row >= 2, pltpu.roll(h, 2, axis=0), jnp.where(row == 1, p1, p2))
        st_ref[0, part] = h[tm - (FFN_K - 1):, :]
        taps = taps_ref[...]
        return taps[0:1, :] * u2 + taps[1:2, :] * u1 + taps[2:3, :] * h

    a = conv_branch(0, wa_ref, ca_ref)
    b = conv_branch(1, wb_ref, cb_ref)
    gated = (_silu(a) * b).astype(wd_ref.dtype)

    @pl.when(j == 0)
    def _():
        acc_ref[...] = jnp.zeros_like(acc_ref)

    acc_ref[...] += _dot(gated, wd_ref[...])

    @pl.when(j == nf - 1)
    def _():
        out = _layer_norm_rows(alpha * res_ref[...] + acc_ref[...], g_ref[...], b_ref[...])
        o_ref[...] = out
        ob_ref[...] = out.astype(ob_ref.dtype)


def _ffn_prompt(x_lo, x, w_up, taps, w_down, g, b, alpha, nb, seq, *, tm_pref=512, tf_pref=512):
    M, D = x.shape
    F = w_down.shape[0]
    tm = _tile(seq, tm_pref, _HALO)
    tf = _tile(F, tf_pref, LANE)
    nf = F // tf
    tps = seq // tm
    hb = tm // _HALO
    out, out_lo, tails = pl.pallas_call(
        functools.partial(_ffn_kernel, alpha, nf, tps),
        out_shape=(jax.ShapeDtypeStruct((M, D), F32), jax.ShapeDtypeStruct((M, D), x_lo.dtype),
                   jax.ShapeDtypeStruct((M // tm, 2, FFN_K - 1, F), F32)),
        grid=(M // tm, nf),
        in_specs=[pl.BlockSpec((tm, D), lambda i, j: (i, 0)),
                  pl.BlockSpec((_HALO, D), lambda i, j: (jnp.maximum(i * hb - 1, 0), 0)),
                  pl.BlockSpec((D, tf), lambda i, j: (0, j)),
                  pl.BlockSpec((D, tf), lambda i, j: (0, nf + j)),
                  pl.BlockSpec((FFN_K, tf), lambda i, j: (0, j)),
                  pl.BlockSpec((FFN_K, tf), lambda i, j: (0, nf + j)),
                  pl.BlockSpec((tf, D), lambda i, j: (j, 0)),
                  pl.BlockSpec((tm, D), lambda i, j: (i, 0)),
                  pl.BlockSpec((1, D), lambda i, j: (0, 0)),
                  pl.BlockSpec((1, D), lambda i, j: (0, 0))],
        out_specs=(pl.BlockSpec((tm, D), lambda i, j: (i, 0)),
                   pl.BlockSpec((tm, D), lambda i, j: (i, 0)),
                   pl.BlockSpec((1, 2, FFN_K - 1, tf), lambda i, j: (i, 0, 0, j))),
        scratch_shapes=[pltpu.VMEM((tm, D), F32)],
        compiler_params=_params(("parallel", "arbitrary")),
        name="ffn_prompt",
    )(x_lo, x_lo, w_up, w_up, taps, taps, w_down, x, g, b)
    state = tails[tps - 1::tps].transpose(0, 2, 1, 3).reshape(nb, FFN_K - 1, 2 * F)
    return out, out_lo, state


def _sample_mix_kernel(nb, z_ref, st_ref, cos_ref, sin_ref, gam_ref, gn_ref, p0_ref, p1_ref, cw_ref,
                       ro_ref, nst_ref, co_ref, u_ref):
    D = RET_DIM
    rowi = lax.broadcasted_iota(I32, (nb, D), 0)
    di = lax.broadcasted_iota(I32, (D, D), 0)
    dj = lax.broadcasted_iota(I32, (D, D), 1)
    cos, sin = cos_ref[...], sin_ref[...]
    for h in range(RET_HEADS):
        sl = lambda c: slice((c * RET_HEADS + h) * D, (c * RET_HEADS + h + 1) * D)
        q, k, v, g = z_ref[:, sl(0)], z_ref[:, sl(1)], z_ref[:, sl(2)], z_ref[:, sl(3)]
        gam = gam_ref[:, h * D:(h + 1) * D]
        qr = q * cos + pltpu.roll(q, D // 2, axis=1) * sin
        kr = (k * cos + pltpu.roll(k, D // 2, axis=1) * sin) * (D ** -0.5)
        qf, kf, vf = (t.astype(BF16).astype(F32) for t in (qr, kr, v))
        inner = jnp.sum(qf * kf, axis=-1, keepdims=True)
        o = inner.astype(BF16).astype(F32) * vf
        cross = jnp.zeros((nb, D), F32)
        for b in range(nb):
            R = st_ref[b, h]
            cross = cross + jnp.where(rowi == b, _dot(qf.astype(BF16), R.astype(BF16)), 0.0)
            kdiag = jnp.where(di == dj, jnp.broadcast_to(kf[b:b + 1, :], (D, D)), 0.0).astype(BF16)
            vrows = jnp.broadcast_to(vf[b:b + 1, :], (D, D)).astype(BF16)
            nst_ref[b, h] = R * gam + _dot(kdiag, vrows)
        o = o + cross * gam
        mu = jnp.mean(o, axis=-1, keepdims=True)
        d = o - mu
        var = jnp.mean(d * d, axis=-1, keepdims=True)
        on = d * lax.rsqrt(var + LN_EPS) * gn_ref[:, h * D:(h + 1) * D]
        ro_ref[:, h * D:(h + 1) * D] = on * _silu(g)
    C = CONV_WIDTH
    cb, cc, ch = (z_ref[:, _Z_CONV + c * C:_Z_CONV + (c + 1) * C] for c in range(3))
    u = cc * ch
    w = cw_ref[...]
    co_ref[...] = cb * (w[0:1, :] * p0_ref[...] + w[1:2, :] * p1_ref[...] + w[2:3, :] * u)
    u_ref[...] = u


def _sample_mix(z, state, cos, sin, gam, gn, p0, p1, cw):
    nb = z.shape[0]
    D, H, C = RET_DIM, RET_HEADS, CONV_WIDTH
    return pl.pallas_call(
        functools.partial(_sample_mix_kernel, nb),
        out_shape=(jax.ShapeDtypeStruct((nb, RET_WIDTH), F32), jax.ShapeDtypeStruct((nb, H, D, D), F32),
                   jax.ShapeDtypeStruct((nb, C), F32), jax.ShapeDtypeStruct((nb, C), F32)),
        name="sample_mix",
        compiler_params=pltpu.CompilerParams(vmem_limit_bytes=_VMEM_LIMIT),
    )(z, state, cos, sin, gam, gn, p0, p1, cw)


def _sample_gate_kernel(F, h_ref, p0_ref, p1_ref, w_ref, o_ref):
    w = w_ref[...]
    y = w[0:1, :] * p0_ref[...] + w[1:2, :] * p1_ref[...] + w[2:3, :] * h_ref[...]
    o_ref[...] = _silu(y[:, :F]) * y[:, F:]


def _sample_gate(h, p0, p1, w):
    nb, F2 = h.shape
    return pl.pallas_call(
        functools.partial(_sample_gate_kernel, F2 // 2),
        out_shape=jax.ShapeDtypeStruct((nb, F2 // 2), F32),
        name="sample_gate",
    )(h, p0, p1, w)


def _prefix_counts(x, upper, lower_strict):
    incl = _dot(x.astype(BF16), upper)
    tot = jnp.broadcast_to(incl[:, LANE - 1:LANE], x.shape)
    return _dot(lower_strict, tot.astype(BF16)) + incl - x


def _sample_score_kernel(layer, topk, n_pages, group, pt_ref, iq_ref, w_ref, ikn_ref, cache_ref,
                         info_ref, buf, sem, key_sc, rank_sc):
    b = pl.program_id(0)

    def page_copy(p):
        return pltpu.make_async_copy(cache_ref.at[layer, pt_ref[b, p]], buf.at[p], sem.at[0])

    def start(p, c):
        page_copy(p).start()
        return c

    def wait(p, c):
        page_copy(p).wait()
        return c

    lax.fori_loop(0, n_pages, start, 0)
    lax.fori_loop(0, n_pages, wait, 0)

    iq = iq_ref[0]
    w = w_ref[0]
    wb = jnp.broadcast_to(w, (IDX_HEADS, group * PAGE_SIZE))

    def score_group(g, carry):
        p0 = g * group
        ik_past = jnp.concatenate([buf[p0 + pp] for pp in range(group)], axis=1).astype(iq.dtype)
        d = _dot(iq, ik_past)
        s = _float_key(jnp.sum(jnp.maximum(d, 0.0) * wb, axis=0, keepdims=True))
        for pp in range(group):
            key_sc[pl.ds(p0 + pp, 1), :] = s[:, pp * PAGE_SIZE:(pp + 1) * PAGE_SIZE]
        return carry

    lax.fori_loop(0, n_pages // group, score_group, 0)
    dn = jnp.sum(iq.astype(F32) * ikn_ref[0].astype(iq.dtype).astype(F32), axis=-1, keepdims=True)
    s_new = jnp.sum(jnp.maximum(dn, 0.0) * w, axis=0, keepdims=True)
    key_new = _float_key(s_new)
    keys = key_sc[...]

    def total(x):
        return jnp.sum(jnp.sum(x, axis=1, keepdims=True), axis=0, keepdims=True)

    def count_ge(cand):
        return total(jnp.where(keys >= cand, 1.0, 0.0)) + jnp.where(key_new >= cand, 1.0, 0.0)

    kf = float(topk)
    t0 = jnp.where(count_ge(jnp.zeros((1, 1), I32)) >= kf, 0, _INT_MIN).astype(I32)

    def bit_step(it, t):
        cand = t | lax.shift_left(jnp.int32(1), jnp.int32(30) - it)
        return jnp.where(count_ge(cand) >= kf, cand, t)

    thr = lax.fori_loop(0, 31, bit_step, t0)

    li = lax.broadcasted_iota(I32, (LANE, LANE), 0)
    lj = lax.broadcasted_iota(I32, (LANE, LANE), 1)
    upper = jnp.where(li <= lj, 1.0, 0.0).astype(BF16)
    pi = lax.broadcasted_iota(I32, (n_pages, n_pages), 0)
    pj = lax.broadcasted_iota(I32, (n_pages, n_pages), 1)
    lower_strict = jnp.where(pj < pi, 1.0, 0.0).astype(BF16)

    gt = jnp.where(keys > thr, 1.0, 0.0)
    eq = jnp.where(keys == thr, 1.0, 0.0)
    need = kf - total(gt) - jnp.where(key_new > thr, 1.0, 0.0)
    sel = gt + eq * jnp.where(_prefix_counts(eq, upper, lower_strict) < need, 1.0, 0.0)
    rank_sc[...] = jnp.where(sel > 0.0, _prefix_counts(sel, upper, lower_strict), -1.0)

    slot = lax.broadcasted_iota(I32, (topk, LANE), 0).astype(F32)
    col = lax.broadcasted_iota(I32, (LANE, LANE), 1)
    base = jnp.where(col == 0, li.astype(F32), jnp.where(col == 2, 1.0, 0.0))

    def compact(p, acc):
        onehot = jnp.where(rank_sc[pl.ds(p, 1), :] == slot, 1.0, 0.0).astype(BF16)
        rhs = jnp.where(col == 1, jnp.asarray(p, F32), base).astype(BF16)
        return acc + _dot(onehot, rhs)

    acc = lax.fori_loop(0, n_pages, compact, jnp.zeros((topk, LANE), F32))
    info_ref[0] = acc.astype(I32)


def _sample_score(page_table, iq, w, ikn, cache_idx_k, layer, topk):
    nb, n_pages = page_table.shape
    group = _tile(n_pages, 8, 1)
    grid_spec = pltpu.PrefetchScalarGridSpec(
        num_scalar_prefetch=1,
        grid=(nb,),
        in_specs=[pl.BlockSpec((1, IDX_HEADS, IDX_DIM), lambda b, pt: (b, 0, 0)),
                  pl.BlockSpec((1, IDX_HEADS, 1), lambda b, pt: (b, 0, 0)),
                  pl.BlockSpec((1, 1, IDX_DIM), lambda b, pt: (b, 0, 0)),
                  pl.BlockSpec(memory_space=pl.ANY)],
        out_specs=pl.BlockSpec((1, topk, LANE), lambda b, pt: (b, 0, 0)),
        scratch_shapes=[pltpu.VMEM((n_pages, IDX_DIM, PAGE_SIZE), F32),
                        pltpu.SemaphoreType.DMA((1,)),
                        pltpu.VMEM((n_pages, PAGE_SIZE), I32),
                        pltpu.VMEM((n_pages, PAGE_SIZE), F32)],
    )
    return pl.pallas_call(
        functools.partial(_sample_score_kernel, layer, topk, n_pages, group),
        out_shape=jax.ShapeDtypeStruct((nb, topk, LANE), I32),
        grid_spec=grid_spec,
        compiler_params=_params(("arbitrary",)),
        name="sample_score",
    )(page_table, iq, w, ikn, cache_idx_k)


def _sample_attn_kernel(layer, topk, pt_ref, page_ref, off_ref, q_ref, kn_ref, vn_ref, inpast_ref,
                        ck_ref, cv_ref, o_ref, kbuf, vbuf, sem):
    b = pl.program_id(0)

    def copies(j):
        src = (layer, pt_ref[b, page_ref[b, j]], off_ref[b, j])
        return (pltpu.make_async_copy(ck_ref.at[src], kbuf.at[j], sem.at[0]),
                pltpu.make_async_copy(cv_ref.at[src], vbuf.at[j], sem.at[1]))

    def start(j, c):
        ck, cv = copies(j)
        ck.start()
        cv.start()
        return c

    def wait(j, c):
        ck, cv = copies(j)
        ck.wait()
        cv.wait()
        return c

    lax.fori_loop(0, topk, start, 0)
    lax.fori_loop(0, topk, wait, 0)

    inpast = inpast_ref[0] > 0
    ks = jnp.where(inpast, kbuf[...], kn_ref[...])
    vs = jnp.where(inpast, vbuf[...], vn_ref[...])
    q = q_ref[...].astype(BF16).astype(F32)
    s = jnp.sum(q * ks.astype(BF16).astype(F32), axis=-1, keepdims=True) * ATT_SCALE
    m = jnp.max(s, axis=0, keepdims=True)
    e = jnp.exp(s - m)
    p = e / jnp.sum(e, axis=0, keepdims=True)
    o_ref[...] = jnp.sum(p.astype(BF16).astype(F32) * vs.astype(BF16).astype(F32), axis=0, keepdims=True)


def _sample_attn(page_table, page, off, q, kn, vn, inpast, cache_k, cache_v, layer, topk):
    nb = q.shape[0]
    H, D = ATT_HEADS, ATT_DIM
    per_b = lambda b, pt, pg, of: (b, 0, 0)
    grid_spec = pltpu.PrefetchScalarGridSpec(
        num_scalar_prefetch=3,
        grid=(nb,),
        in_specs=[pl.BlockSpec((1, H, D), per_b), pl.BlockSpec((1, H, D), per_b),
                  pl.BlockSpec((1, H, D), per_b),
                  pl.BlockSpec((1, topk, 1, 1), lambda b, pt, pg, of: (b, 0, 0, 0)),
                  pl.BlockSpec(memory_space=pl.ANY), pl.BlockSpec(memory_space=pl.ANY)],
        out_specs=pl.BlockSpec((1, H, D), per_b),
        scratch_shapes=[pltpu.VMEM((topk, H, D), F32), pltpu.VMEM((topk, H, D), F32),
                        pltpu.SemaphoreType.DMA((2,))],
    )
    return pl.pallas_call(
        functools.partial(_sample_attn_kernel, layer, topk),
        out_shape=jax.ShapeDtypeStruct((nb, H, D), F32),
        grid_spec=grid_spec,
        compiler_params=_params(("arbitrary",)),
        name="sample_attn",
    )(page_table, page, off, q, kn, vn, inpast, cache_k, cache_v)


def kernel(x_prompt, x_sample, cache_k, cache_v, cache_idx_k, state_ret, state_conv, state_ffn,
           page_table, w_in, ret_gn_g, idx_kn_g, idx_kn_b, conv_w, w_out, ln1_g, ln1_b,
           w_up, ffn_conv_w, w_down, ln2_g, ln2_b):
    B, S, D = x_prompt.shape
    DB, T, _ = x_sample.shape
    assert T == 1 and S % PAGE_SIZE == 0 and w_in.shape[-1] == _D_IN
    depth = w_in.shape[0]
    n_pages = page_table.shape[1]
    past = n_pages * PAGE_SIZE
    F2 = w_up.shape[-1]
    alpha = (2 * depth) ** 0.25
    topk_s = min(TOPK_MAX, (past + T) // 4)

    pos_p = jnp.arange(S, dtype=jnp.int32)
    pos_s = past + jnp.arange(T, dtype=jnp.int32)
    att_tabs_p = _rope_tables(pos_p, ROT_DIM, ROPE_THETA, ATT_DIM)
    idx_tabs_p = _rope_tables(pos_p, IDX_ROT_DIM, ROPE_THETA, IDX_DIM)
    rc, ru, rd = _rope_tables(pos_p, RET_DIM, RET_THETA, RET_DIM)
    ret_cos_p, ret_sin_p = rc, ru + rd
    bcast = lambda t: jnp.broadcast_to(t, (DB, LANE))
    att_tabs_s = tuple(bcast(t) for t in _rope_tables(pos_s, ROT_DIM, ROPE_THETA, ATT_DIM))
    idx_tabs_s = tuple(bcast(t) for t in _rope_tables(pos_s, IDX_ROT_DIM, ROPE_THETA, IDX_DIM))
    rc, ru, rd = _rope_tables(pos_s, RET_DIM, RET_THETA, RET_DIM)
    ret_cos_s, ret_sin_s = bcast(rc), bcast(ru + rd)
    gamma = 1.0 - 2.0 ** (-5.0 - jnp.arange(RET_HEADS, dtype=F32))
    gamma = jnp.exp(jnp.log(gamma))
    gam_row = jnp.repeat(gamma, RET_DIM)[None, :]

    cache_idx_t = jnp.swapaxes(cache_idx_k, 2, 3)
    xp = x_prompt.reshape(B * S, D)
    xp_lo = xp.astype(BF16)
    xs = x_sample.reshape(DB, D)

    outs_p = [[] for _ in range(6)]
    outs_s = [[] for _ in range(6)]
    row2 = lambda v: v.reshape(1, -1)
    pad_small = lambda v: jnp.pad(v, (0, _SMALL_W - v.shape[0])).reshape(1, _SMALL_W)

    for l in range(depth):
        wl = w_in[l]
        w_main = jnp.concatenate([wl[:, :_O_IK], wl[:, _O_CONV:]], axis=1).astype(BF16)
        w_small = jnp.pad(wl[:, _O_IK:_O_CONV], ((0, 0), (0, _SMALL_W - (_O_CONV - _O_IK)))).astype(BF16)
        w_out_l, w_up_l, w_down_l = w_out[l].astype(BF16), w_up[l].astype(BF16), w_down[l].astype(BF16)
        gn = row2(ret_gn_g[l])
        kg, kb_ = pad_small(idx_kn_g[l]), pad_small(idx_kn_b[l])
        g1, b1, g2, b2 = row2(ln1_g[l]), row2(ln1_b[l]), row2(ln2_g[l]), row2(ln2_b[l])

        z, zs = _inproj(xp_lo, w_main, w_small)
        ro, p_ret = _retention_prompt(z, ret_cos_p, ret_sin_p, gn, B, S)
        qb, kb, vT, kf, iqb, ikb, smT = _prep(z, zs, att_tabs_p, idx_tabs_p, kg, kb_, S, transposed=True)
        ao = _dsa_prompt(iqb, ikb, smT, qb, kb, vT, B, S)
        co, p_conv = _conv_prompt(z, conv_w[l], B, S)
        x1, x1_lo = _outproj_ln(ro, ao, co, w_out_l, xp, g1, b1, alpha)
        xp, xp_lo, p_ffn = _ffn_prompt(x1_lo, x1, w_up_l, ffn_conv_w[l], w_down_l, g2, b2, alpha, B, S)
        outs_p[0].append(kf.reshape(B, S, ATT_HEADS, ATT_DIM))
        outs_p[1].append(z[:, _O_ATT + 2 * ATT_WIDTH:_O_ATT + 3 * ATT_WIDTH].reshape(B, S, ATT_HEADS, ATT_DIM))
        outs_p[2].append(smT[:IDX_DIM, :].T.reshape(B, S, IDX_DIM))
        outs_p[3].append(p_ret)
        outs_p[4].append(p_conv)
        outs_p[5].append(p_ffn)

        z, zs = _inproj(xs, w_main, w_small)
        ro, s_ret, co, u_new = _sample_mix(z, state_ret[l], ret_cos_s, ret_sin_s, gam_row, gn,
                                           state_conv[l, :, 0], state_conv[l, :, 1], conv_w[l])
        q_s, _, _, k_s, iq_s, _, sm_s = _prep(z, zs, att_tabs_s, idx_tabs_s, kg, kb_, DB,
                                              transposed=False, lowp_dtype=F32)
        v_s = z[:, _O_ATT + 2 * ATT_WIDTH:_O_ATT + 3 * ATT_WIDTH]
        ik_s = sm_s[:, :IDX_DIM]
        iw_s = sm_s[:, IDX_DIM:IDX_DIM + IDX_HEADS]
        info = _sample_score(page_table, iq_s.reshape(DB, IDX_HEADS, IDX_DIM).astype(BF16),
                             iw_s.reshape(DB, IDX_HEADS, 1), ik_s.reshape(DB, 1, IDX_DIM),
                             cache_idx_t, l, topk_s)
        hd = lambda t: t.reshape(DB, ATT_HEADS, ATT_DIM)
        ao = _sample_attn(page_table, info[:, :, 1], info[:, :, 0], hd(q_s), hd(k_s), hd(v_s),
                          info[:, :, 2].reshape(DB, topk_s, 1, 1), cache_k, cache_v, l, topk_s)
        x1, _ = _outproj_ln(ro, ao.reshape(DB, ATT_WIDTH), co, w_out_l, xs, g1, b1, alpha, lowp_dtype=F32)
        h = _matmul(x1, w_up_l, name="up_s")
        gt = _sample_gate(h, state_ffn[l, :, 0], state_ffn[l, :, 1], ffn_conv_w[l])
        xs_new, _ = _down_ln(gt, w_down_l, x1, g2, b2, alpha, lowp_dtype=F32)
        outs_s[0].append(k_s.reshape(DB, T, ATT_HEADS, ATT_DIM))
        outs_s[1].append(v_s.reshape(DB, T, ATT_HEADS, ATT_DIM))
        outs_s[2].append(ik_s.reshape(DB, T, IDX_DIM))
        outs_s[3].append(s_ret)
        outs_s[4].append(jnp.stack([state_conv[l, :, 1], u_new], axis=1))
        outs_s[5].append(jnp.stack([state_ffn[l, :, 1], h], axis=1))
        xs = xs_new

    n_pp = S // PAGE_SIZE
    st = lambda xs_: jnp.stack(xs_)
    return (xp.reshape(B, S, D), xs.reshape(DB, T, D),
            st(outs_p[0]).reshape(depth, B, n_pp, PAGE_SIZE, ATT_HEADS, ATT_DIM),
            st(outs_p[1]).reshape(depth, B, n_pp, PAGE_SIZE, ATT_HEADS, ATT_DIM),
            st(outs_p[2]).reshape(depth, B, n_pp, PAGE_SIZE, IDX_DIM),
            st(outs_p[3]), st(outs_p[4]), st(outs_p[5]),
            st(outs_s[0]), st(outs_s[1]), st(outs_s[2]), st(outs_s[3]), st(outs_s[4]), st(outs_s[5]))
```

```python
import functools

import numpy as np
import jax
import jax.numpy as jnp
from jax import lax
from jax.experimental import pallas as pl
from jax.experimental.pallas import tpu as pltpu

F32 = jnp.float32
BF16 = jnp.bfloat16
I32 = jnp.int32

PAGE_SIZE = 128
RET_HEADS, RET_DIM, RET_CHUNK, RET_THETA = 4, 128, 128, 10000.0
RET_WIDTH = RET_HEADS * RET_DIM
ATT_HEADS, ATT_DIM = 8, 128
ATT_WIDTH = ATT_HEADS * ATT_DIM
ROT_DIM = ATT_DIM // 4
ROPE_THETA = 500000.0
IDX_HEADS, IDX_DIM = 16, 64
IDX_WIDTH = IDX_HEADS * IDX_DIM
IDX_ROT_DIM = IDX_DIM // 4
TOPK_MAX = 256
ATT_SCALE = ATT_DIM ** -0.5
IDX_W_SCALE = (IDX_HEADS * IDX_DIM) ** -0.5
CONV_WIDTH, CONV_K = 512, 3
FFN_K = 3
LN_EPS = 1e-5

_O_RET = 0
_O_ATT = 4 * RET_WIDTH
_O_IQ = _O_ATT + 3 * ATT_WIDTH
_O_IK = _O_IQ + IDX_WIDTH
_O_IW = _O_IK + IDX_DIM
_O_CONV = _O_IW + IDX_HEADS
_D_IN = _O_CONV + 3 * CONV_WIDTH
_Z_CONV = _O_IK
_Z_MAIN = _Z_CONV + 3 * CONV_WIDTH
LANE = 128
_SMALL_W = LANE

_NEG_BIG = -1e30
_INT_MIN = -2 ** 31
_KEY_NEG_INF = int(np.int32(np.uint32(0xFF800000 ^ 0x7FFFFFFF)))

_VMEM_LIMIT = 56 * 1024 * 1024


def _params(sem):
    return pltpu.CompilerParams(dimension_semantics=sem, vmem_limit_bytes=_VMEM_LIMIT)


def _tile(dim, pref, mult):
    t = min(pref, dim)
    t -= t % mult
    while t >= mult:
        if dim % t == 0:
            return t
        t -= mult
    return dim


def _float_key(s):
    b = lax.bitcast_convert_type(s, I32)
    return b ^ ((b >> 31) & jnp.int32(0x7FFFFFFF))


def _layer_norm_rows(r, g, b):
    mu = jnp.mean(r, axis=-1, keepdims=True)
    d = r - mu
    var = jnp.mean(d * d, axis=-1, keepdims=True)
    return d * lax.rsqrt(var + LN_EPS) * g + b


def _silu(x):
    return x * (1.0 / (1.0 + jnp.exp(-x)))


def _dot(a, b):
    return jnp.dot(a, b, preferred_element_type=F32)


def _dot_nt(a, b):
    return lax.dot_general(a, b, (((1,), (1,)), ((), ())), preferred_element_type=F32)


def _mm_kernel(x_ref, w_ref, o_ref):
    o_ref[...] = _dot(x_ref[...].astype(BF16), w_ref[...]).astype(o_ref.dtype)


def _matmul(x, w, layer, *, tm_pref=1024, tn_pref=512, out_dtype=F32, name="mm"):
    M, K = x.shape
    N = w.shape[2]
    tm = _tile(M, tm_pref, 8)
    tn = _tile(N, tn_pref, LANE)
    return pl.pallas_call(
        _mm_kernel,
        out_shape=jax.ShapeDtypeStruct((M, N), out_dtype),
        grid=(M // tm, N // tn),
        in_specs=[pl.BlockSpec((tm, K), lambda i, j: (i, 0)),
                  pl.BlockSpec((None, K, tn), lambda i, j: (layer, 0, j))],
        out_specs=pl.BlockSpec((tm, tn), lambda i, j: (i, j)),
        compiler_params=_params(("parallel", "arbitrary")),
        name=name,
    )(x, w)


def _inproj_kernel(x_ref, w_ref, ws_ref, o_ref, os_ref):
    x = x_ref[...].astype(BF16)
    o_ref[...] = _dot(x, w_ref[...])

    @pl.when(pl.program_id(1) == 0)
    def _():
        os_ref[...] = _dot(x, ws_ref[...])


def _inproj(x, w_main, w_small, layer, *, tm_pref=1024, tn_pref=512):
    M, K = x.shape
    N = w_main.shape[2]
    tm = _tile(M, tm_pref, 8)
    tn = _tile(N, tn_pref, LANE)
    return pl.pallas_call(
        _inproj_kernel,
        out_shape=(jax.ShapeDtypeStruct((M, N), F32), jax.ShapeDtypeStruct((M, _SMALL_W), F32)),
        grid=(M // tm, N // tn),
        in_specs=[pl.BlockSpec((tm, K), lambda i, j: (i, 0)),
                  pl.BlockSpec((None, K, tn), lambda i, j: (layer, 0, j)),
                  pl.BlockSpec((None, K, _SMALL_W), lambda i, j: (layer, 0, 0))],
        out_specs=(pl.BlockSpec((tm, tn), lambda i, j: (i, j)),
                   pl.BlockSpec((tm, _SMALL_W), lambda i, j: (i, 0))),
        compiler_params=_params(("parallel", "arbitrary")),
        name="inproj",
    )(x, w_main, w_small)


def _outproj_ln_kernel(alpha, ro_ref, ao_ref, co_ref, w_ref, x_ref, g_ref, b_ref, o_ref, ob_ref):
    y = _dot(ro_ref[...].astype(BF16), w_ref[0:RET_WIDTH, :])
    y += _dot(ao_ref[...].astype(BF16), w_ref[RET_WIDTH:RET_WIDTH + ATT_WIDTH, :])
    y += _dot(co_ref[...].astype(BF16), w_ref[RET_WIDTH + ATT_WIDTH:, :])
    out = _layer_norm_rows(alpha * x_ref[...] + y, g_ref[...], b_ref[...])
    o_ref[...] = out
    ob_ref[...] = out.astype(ob_ref.dtype)


def _outproj_ln(ro, ao, co, w_out, layer, x, g, b, alpha, *, tm_pref=512, lowp_dtype=BF16):
    M, D = x.shape
    tm = _tile(M, tm_pref, 8)
    row = lambda i: (i, 0)
    fixed = lambda i: (0, 0)
    return pl.pallas_call(
        functools.partial(_outproj_ln_kernel, alpha),
        out_shape=(jax.ShapeDtypeStruct((M, D), F32), jax.ShapeDtypeStruct((M, D), lowp_dtype)),
        grid=(M // tm,),
        in_specs=[pl.BlockSpec((tm, RET_WIDTH), row), pl.BlockSpec((tm, ATT_WIDTH), row),
                  pl.BlockSpec((tm, CONV_WIDTH), row),
                  pl.BlockSpec((None,) + w_out.shape[1:], lambda i: (layer, 0, 0)),
                  pl.BlockSpec((tm, D), row), pl.BlockSpec((1, D), fixed), pl.BlockSpec((1, D), fixed)],
        out_specs=(pl.BlockSpec((tm, D), row), pl.BlockSpec((tm, D), row)),
        compiler_params=_params(("parallel",)),
        name="outproj_ln",
    )(ro, ao, co, w_out, x, g, b)


def _down_ln_kernel(alpha, nk, gt_ref, w_ref, x_ref, g_ref, b_ref, o_ref, ob_ref, acc_ref):
    k = pl.program_id(1)

    @pl.when(k == 0)
    def _():
        acc_ref[...] = jnp.zeros_like(acc_ref)

    acc_ref[...] += _dot(gt_ref[...].astype(BF16), w_ref[...])

    @pl.when(k == nk - 1)
    def _():
        out = _layer_norm_rows(alpha * x_ref[...] + acc_ref[...], g_ref[...], b_ref[...])
        o_ref[...] = out
        ob_ref[...] = out.astype(ob_ref.dtype)


def _down_ln(gt, w_down, layer, x, g, b, alpha, *, tm_pref=512, tk_pref=512, lowp_dtype=BF16):
    M, D = x.shape
    K = gt.shape[1]
    tm = _tile(M, tm_pref, 8)
    tk = _tile(K, tk_pref, LANE)
    nk = K // tk
    return pl.pallas_call(
        functools.partial(_down_ln_kernel, alpha, nk),
        out_shape=(jax.ShapeDtypeStruct((M, D), F32), jax.ShapeDtypeStruct((M, D), lowp_dtype)),
        grid=(M // tm, nk),
        in_specs=[pl.BlockSpec((tm, tk), lambda i, k: (i, k)),
                  pl.BlockSpec((None, tk, D), lambda i, k: (layer, k, 0)),
                  pl.BlockSpec((tm, D), lambda i, k: (i, 0)),
                  pl.BlockSpec((1, D), lambda i, k: (0, 0)),
                  pl.BlockSpec((1, D), lambda i, k: (0, 0))],
        out_specs=(pl.BlockSpec((tm, D), lambda i, k: (i, 0)),
                   pl.BlockSpec((tm, D), lambda i, k: (i, 0))),
        scratch_shapes=[pltpu.VMEM((tm, D), F32)],
        compiler_params=_params(("parallel", "arbitrary")),
        name="down_ln",
    )(gt, w_down, x, g, b)


def _rope_tables(pos, rot_dim, theta, period):
    half = rot_dim // 2
    inv = theta ** (-jnp.arange(half, dtype=F32) / half)
    ang = pos.astype(F32)[:, None] * inv[None, :]
    cos, sin = jnp.cos(ang), jnp.sin(ang)
    L = pos.shape[0]
    rest = period - rot_dim
    c = jnp.concatenate([cos, cos, jnp.ones((L, rest), F32)], axis=1)
    s_up = jnp.concatenate([-sin, jnp.zeros((L, half + rest), F32)], axis=1)
    s_dn = jnp.concatenate([jnp.zeros((L, half), F32), sin, jnp.zeros((L, rest), F32)], axis=1)
    rep = LANE // period
    return tuple(jnp.tile(t, (1, rep)) for t in (c, s_up, s_dn))


def _rope128(x, c, s_up, s_dn, half):
    if 2 * half == LANE:
        return x * c + pltpu.roll(x, half, axis=1) * (s_up + s_dn)
    return x * c + pltpu.roll(x, LANE - half, axis=1) * s_up + pltpu.roll(x, half, axis=1) * s_dn


def _prep_kernel(transposed, aq_ref, ak_ref, av_ref, iq_ref, sm_ref, ac_ref, au_ref, ad_ref,
                 ic_ref, iu_ref, id_ref, kg_ref, kb_ref,
                 q_o, k_o, v_o, kf_o, iq_o, ik_o, sm_o):
    ac, au, ad = ac_ref[...], au_ref[...], ad_ref[...]
    ic, iu, idn = ic_ref[...], iu_ref[...], id_ref[...]
    for h in range(ATT_HEADS):
        sl = slice(h * ATT_DIM, (h + 1) * ATT_DIM)
        q_o[:, sl] = _rope128(aq_ref[:, sl], ac, au, ad, ROT_DIM // 2).astype(q_o.dtype)
        kr = _rope128(ak_ref[:, sl], ac, au, ad, ROT_DIM // 2)
        kf_o[:, sl] = kr
        k_o[:, sl] = kr.astype(k_o.dtype)
        if transposed:
            v_o[sl, :] = av_ref[:, sl].T.astype(v_o.dtype)
    if not transposed:
        v_o[...] = av_ref[...].astype(v_o.dtype)
    for c in range(IDX_WIDTH // LANE):
        sl = slice(c * LANE, (c + 1) * LANE)
        iq_o[:, sl] = _rope128(iq_ref[:, sl], ic, iu, idn, IDX_ROT_DIM // 2).astype(iq_o.dtype)
    sm = sm_ref[...]
    lane = lax.broadcasted_iota(I32, sm.shape, 1)
    is_k = lane < IDX_DIM
    mu = jnp.sum(jnp.where(is_k, sm, 0.0), axis=-1, keepdims=True) * (1.0 / IDX_DIM)
    d = jnp.where(is_k, sm - mu, 0.0)
    var = jnp.sum(d * d, axis=-1, keepdims=True) * (1.0 / IDX_DIM)
    n = d * lax.rsqrt(var + LN_EPS) * kg_ref[...] + kb_ref[...]
    roped = _rope128(n, ic, iu, idn, IDX_ROT_DIM // 2)
    out = jnp.where(is_k, roped, jnp.where(lane < IDX_DIM + IDX_HEADS, sm * IDX_W_SCALE, 0.0))
    sm_o[...] = out.T if transposed else out
    ik_o[...] = out.astype(ik_o.dtype)


def _prep(z, zs, att_tabs, idx_tabs, kg, kb, seq, *, transposed, tm_pref=256, lowp_dtype=BF16):
    M = z.shape[0]
    tm = _tile(seq, tm_pref, 8)
    nt = seq // tm
    col = lambda c: (lambda i: (i, c))
    tab = lambda i: (i % nt, 0)
    fixed = lambda i: (0, 0)
    W = ATT_WIDTH
    big = lambda dt: jax.ShapeDtypeStruct((M, W), dt)
    rows, small = pl.BlockSpec((tm, W), col(0)), pl.BlockSpec((tm, LANE), col(0))
    if transposed:
        v_shape, v_spec = jax.ShapeDtypeStruct((W, M), lowp_dtype), pl.BlockSpec((W, tm), lambda i: (0, i))
        sm_shape, sm_spec = jax.ShapeDtypeStruct((LANE, M), F32), pl.BlockSpec((LANE, tm), lambda i: (0, i))
    else:
        v_shape, v_spec = big(lowp_dtype), rows
        sm_shape, sm_spec = jax.ShapeDtypeStruct((M, LANE), F32), small
    return pl.pallas_call(
        functools.partial(_prep_kernel, transposed),
        out_shape=(big(lowp_dtype), big(lowp_dtype), v_shape, big(F32), big(lowp_dtype),
                   jax.ShapeDtypeStruct((M, LANE), lowp_dtype), sm_shape),
        grid=(M // tm,),
        in_specs=[pl.BlockSpec((tm, W), col(_O_ATT // W)), pl.BlockSpec((tm, W), col(_O_ATT // W + 1)),
                  pl.BlockSpec((tm, W), col(_O_ATT // W + 2)), pl.BlockSpec((tm, W), col(_O_IQ // W)),
                  pl.BlockSpec((tm, LANE), col(0))]
        + [pl.BlockSpec((tm, LANE), tab)] * 6
        + [pl.BlockSpec((1, LANE), fixed)] * 2,
        out_specs=(rows, rows, v_spec, rows, rows, small, sm_spec),
        compiler_params=_params(("parallel",)),
        name="prep",
    )(z, z, z, z, zs, *att_tabs, *idx_tabs, kg, kb)


def _fold8(x, op):
    return op(x.reshape(x.shape[0] // 8, 8, x.shape[1]), axis=0)


def _dsa_kernel(topk, tq, iq_ref, ik_ref, smT_ref, q_ref, k_ref, vT_ref, o_ref,
                key_sc, bias_sc, s_sc, acc_sc):
    tk = tq
    i = pl.program_id(1)
    nkb = i + 1
    qpos = i * tq + lax.broadcasted_iota(I32, (tk, tq), 1)

    def score_block(kb, carry):
        k0 = pl.multiple_of(kb * tk, tk)
        ikb = ik_ref[pl.ds(k0, tk), 0:IDX_DIM]
        acc = jnp.zeros((tk, tq), F32)
        for h in range(IDX_HEADS):
            d = _dot_nt(ikb, iq_ref[:, h * IDX_DIM:(h + 1) * IDX_DIM])
            acc = acc + jnp.maximum(d, 0.0) * smT_ref[IDX_DIM + h:IDX_DIM + h + 1, :]
        kpos = k0 + lax.broadcasted_iota(I32, (tk, tq), 0)
        key_sc[kb] = _float_key(jnp.where(kpos <= qpos, acc, -jnp.inf))
        return carry

    lax.fori_loop(0, nkb, score_block, 0)

    def count_ge(cand):
        def body(kb, part):
            return part + _fold8(jnp.where(key_sc[kb] >= cand, 1.0, 0.0), jnp.sum)
        part = lax.fori_loop(0, nkb, body, jnp.zeros((8, tq), F32))
        return jnp.sum(part, axis=0, keepdims=True)

    kf = float(topk)
    t0 = jnp.where(count_ge(jnp.zeros((1, tq), I32)) >= kf, 0, _INT_MIN).astype(I32)

    def bit_step(it, t):
        cand = t | lax.shift_left(jnp.int32(1), jnp.int32(30) - it)
        return jnp.where(count_ge(cand) >= kf, cand, t)

    thr = lax.fori_loop(0, 31, bit_step, t0)
    thr = jnp.maximum(thr, _KEY_NEG_INF + 1)
    has_ties = jnp.max(count_ge(thr)) > kf

    @pl.when(jnp.logical_not(has_ties))
    def _():
        def bias_block(kb, carry):
            bias_sc[kb] = jnp.where(key_sc[kb] >= thr, 0.0, _NEG_BIG)
            return carry

        lax.fori_loop(0, nkb, bias_block, 0)

    @pl.when(has_ties)
    def _():
        need = kf - count_ge(thr + 1)
        ki = lax.broadcasted_iota(I32, (tk, tk), 0)
        kj = lax.broadcasted_iota(I32, (tk, tk), 1)
        lower_strict = jnp.where(kj < ki, 1.0, 0.0).astype(BF16)

        def bias_block(kb, before):
            keys = key_sc[kb]
            eq = jnp.where(keys == thr, 1.0, 0.0)
            rank = _dot(lower_strict, eq.astype(BF16)) + before
            take = jnp.where(keys > thr, 1.0, eq * jnp.where(rank < need, 1.0, 0.0))
            bias_sc[kb] = jnp.where(take > 0.0, 0.0, _NEG_BIG)
            return before + jnp.sum(_fold8(eq, jnp.sum), axis=0, keepdims=True)

        lax.fori_loop(0, nkb, bias_block, jnp.zeros((1, tq), F32))

    heads = [slice(h * ATT_DIM, (h + 1) * ATT_DIM) for h in range(ATT_HEADS)]

    def scores_pass(kb, mparts):
        k0 = pl.multiple_of(kb * tk, tk)
        bias = bias_sc[kb]
        new = []
        for h, sl in enumerate(heads):
            s = _dot_nt(k_ref[pl.ds(k0, tk), sl], q_ref[:, sl]) + bias
            s_sc[h, kb] = s
            new.append(jnp.maximum(mparts[h], _fold8(s, jnp.max)))
        return tuple(new)

    mparts = lax.fori_loop(0, nkb, scores_pass,
                           tuple(jnp.full((8, tq), _NEG_BIG, F32) for _ in heads))
    ms = [jnp.max(mp, axis=0, keepdims=True) for mp in mparts]
    acc_sc[...] = jnp.zeros(acc_sc.shape, F32)
    c = ATT_SCALE * float(np.log2(np.e))

    def values_pass(kb, lparts):
        k0 = pl.multiple_of(kb * tk, tk)
        new = []
        for h, sl in enumerate(heads):
            p = jnp.exp2((s_sc[h, kb] - ms[h]) * c)
            new.append(lparts[h] + _fold8(p, jnp.sum))
            acc_sc[sl, :] += _dot(vT_ref[sl, pl.ds(k0, tk)], p.astype(vT_ref.dtype))
        return tuple(new)

    lparts = lax.fori_loop(0, nkb, values_pass, tuple(jnp.zeros((8, tq), F32) for _ in heads))
    for h, sl in enumerate(heads):
        l = jnp.sum(lparts[h], axis=0, keepdims=True)
        o_ref[:, sl] = (acc_sc[sl, :] / l).T.astype(o_ref.dtype)


def _dsa_prompt(iqb, ikb, smT, qb, kb, vT, nb, seq, *, tq_pref=256, out_dtype=BF16):
    topk = min(TOPK_MAX, seq // 4)
    tq = _tile(seq, tq_pref, LANE)
    nq = seq // tq
    W = ATT_WIDTH
    qblk = lambda b, i: (b * nq + i, 0)
    seqblk = lambda b, i: (b, 0)
    return pl.pallas_call(
        functools.partial(_dsa_kernel, topk, tq),
        out_shape=jax.ShapeDtypeStruct((nb * seq, W), out_dtype),
        grid=(nb, nq),
        in_specs=[pl.BlockSpec((tq, IDX_WIDTH), qblk), pl.BlockSpec((seq, LANE), seqblk),
                  pl.BlockSpec((LANE, tq), lambda b, i: (0, b * nq + i)), pl.BlockSpec((tq, W), qblk),
                  pl.BlockSpec((seq, W), seqblk), pl.BlockSpec((W, seq), lambda b, i: (0, b))],
        out_specs=pl.BlockSpec((tq, W), qblk),
        scratch_shapes=[pltpu.VMEM((nq, tq, tq), I32), pltpu.VMEM((nq, tq, tq), F32),
                        pltpu.VMEM((ATT_HEADS, nq, tq, tq), F32), pltpu.VMEM((W, tq), F32)],
        compiler_params=_params(("parallel", "arbitrary")),
        name="dsa_prompt",
    )(iqb, ikb, smT, qb, kb, vT)


def _ret_tables(chunk):
    lg = jnp.log(1.0 - 2.0 ** (-5.0 - jnp.arange(RET_HEADS, dtype=F32)))
    i = jnp.arange(chunk, dtype=F32)
    diff = i[:, None] - i[None, :]
    decay = jnp.where(diff >= 0, jnp.exp(lg[:, None, None] * jnp.maximum(diff, 0.0)), 0.0)
    xi = jnp.exp(lg[:, None] * (i + 1.0))[:, :, None]
    zeta = jnp.exp(lg[:, None] * (chunk - 1.0 - i))[:, :, None]
    g_c = jnp.exp(lg * chunk)[:, None, None]
    return decay, xi, zeta, g_c


def _ret_kernel(nchunk, chunk, q_ref, k_ref, v_ref, g_ref, cos_ref, sin_ref, dec_ref, xi_ref,
                zeta_ref, gc_ref, gn_ref, o_ref, st_ref):
    D = RET_DIM
    st_ref[...] = jnp.zeros(st_ref.shape, F32)

    def step(c, carry):
        r0 = pl.multiple_of(c * chunk, chunk)
        rows = pl.ds(r0, chunk)
        cos, sin = cos_ref[rows, :], sin_ref[rows, :]
        for h in range(RET_HEADS):
            sl = slice(h * D, (h + 1) * D)
            q = q_ref[rows, sl]
            k = k_ref[rows, sl]
            qr = q * cos + pltpu.roll(q, D // 2, axis=1) * sin
            kr = (k * cos + pltpu.roll(k, D // 2, axis=1) * sin) * (D ** -0.5)
            qb, vb = qr.astype(BF16), v_ref[rows, sl].astype(BF16)
            R = st_ref[0, h]
            inner = _dot_nt(qb, kr.astype(BF16)) * dec_ref[h]
            o = _dot(inner.astype(BF16), vb) + _dot(qb, R.astype(BF16)) * xi_ref[h]
            kz = (kr * zeta_ref[h]).astype(BF16)
            st_ref[0, h] = R * gc_ref[h] + lax.dot_general(kz, vb, (((0,), (0,)), ((), ())),
                                                           preferred_element_type=F32)
            mu = jnp.mean(o, axis=-1, keepdims=True)
            d = o - mu
            var = jnp.mean(d * d, axis=-1, keepdims=True)
            on = d * lax.rsqrt(var + LN_EPS) * gn_ref[:, sl]
            o_ref[rows, sl] = (on * _silu(g_ref[rows, sl])).astype(o_ref.dtype)
        return carry

    lax.fori_loop(0, nchunk, step, 0)


def _retention_prompt(z, cos, sin, gn, nb, seq, *, out_dtype=BF16):
    chunk = RET_CHUNK if seq % RET_CHUNK == 0 else seq
    nchunk = seq // chunk
    decay, xi, zeta, g_c = _ret_tables(chunk)
    D, H, W = RET_DIM, RET_HEADS, RET_WIDTH
    col = lambda c: (lambda b: (b, c))
    tab = lambda b: (0, 0)
    whole = lambda b: (0, 0, 0)
    return pl.pallas_call(
        functools.partial(_ret_kernel, nchunk, chunk),
        out_shape=(jax.ShapeDtypeStruct((nb * seq, W), out_dtype),
                   jax.ShapeDtypeStruct((nb, H, D, D), F32)),
        grid=(nb,),
        in_specs=[pl.BlockSpec((seq, W), col(0)), pl.BlockSpec((seq, W), col(1)),
                  pl.BlockSpec((seq, W), col(2)), pl.BlockSpec((seq, W), col(3)),
                  pl.BlockSpec((seq, D), tab), pl.BlockSpec((seq, D), tab),
                  pl.BlockSpec((H, chunk, chunk), whole), pl.BlockSpec((H, chunk, 1), whole),
                  pl.BlockSpec((H, chunk, 1), whole), pl.BlockSpec((H, 1, 1), whole),
                  pl.BlockSpec((1, W), tab)],
        out_specs=(pl.BlockSpec((seq, W), lambda b: (b, 0)),
                   pl.BlockSpec((1, H, D, D), lambda b: (b, 0, 0, 0))),
        compiler_params=_params(("parallel",)),
        name="retention_prompt",
    )(z, z, z, z, cos, sin, decay, xi, zeta, g_c, gn)


def _causal_conv3_zero_prefix(u, w):
    row = lax.broadcasted_iota(I32, u.shape, 0)
    u1 = jnp.where(row >= 1, pltpu.roll(u, 1, axis=0), 0.0)
    u2 = jnp.where(row >= 2, pltpu.roll(u, 2, axis=0), 0.0)
    return w[0:1, :] * u2 + w[1:2, :] * u1 + w[2:3, :] * u


def _conv_kernel(seq, cb_ref, cc_ref, ch_ref, w_ref, o_ref, st_ref):
    u = cc_ref[...] * ch_ref[...]
    o_ref[...] = (cb_ref[...] * _causal_conv3_zero_prefix(u, w_ref[...])).astype(o_ref.dtype)
    st_ref[0] = u[seq - (CONV_K - 1):, :]


def _conv_prompt(z, w, nb, seq, *, out_dtype=BF16):
    C = CONV_WIDTH
    c0 = _Z_CONV // C
    col = lambda c: (lambda b: (b, c0 + c))
    return pl.pallas_call(
        functools.partial(_conv_kernel, seq),
        out_shape=(jax.ShapeDtypeStruct((nb * seq, C), out_dtype),
                   jax.ShapeDtypeStruct((nb, CONV_K - 1, C), F32)),
        grid=(nb,),
        in_specs=[pl.BlockSpec((seq, C), col(0)), pl.BlockSpec((seq, C), col(1)),
                  pl.BlockSpec((seq, C), col(2)), pl.BlockSpec((CONV_K, C), lambda b: (0, 0))],
        out_specs=(pl.BlockSpec((seq, C), lambda b: (b, 0)),
                   pl.BlockSpec((1, CONV_K - 1, C), lambda b: (b, 0, 0))),
        compiler_params=_params(("parallel",)),
        name="conv_prompt",
    )(z, z, z, w)


_HALO = 16


def _upgate_kernel(tiles_per_seq, x_ref, halo_ref, wa_ref, wb_ref, ca_ref, cb_ref, o_ref, st_ref):
    i = pl.program_id(0)
    tm, tf = x_ref.shape[0], wa_ref.shape[1]
    x, xh = x_ref[...], halo_ref[...]
    seq_start = (i % tiles_per_seq) == 0
    row = lax.broadcasted_iota(I32, (tm, tf), 0)

    def conv_branch(part, w_ref, taps_ref):
        h = _dot(x, w_ref[...])
        hh = _dot(xh, w_ref[...])
        p1 = jnp.where(seq_start, 0.0, hh[_HALO - 1:_HALO, :])
        p2 = jnp.where(seq_start, 0.0, hh[_HALO - 2:_HALO - 1, :])
        u1 = jnp.where(row >= 1, pltpu.roll(h, 1, axis=0), p1)
        u2 = jnp.where(row >= 2, pltpu.roll(h, 2, axis=0), jnp.where(row == 1, p1, p2))
        st_ref[0, part] = h[tm - (FFN_K - 1):, :]
        taps = taps_ref[...]
        return taps[0:1, :] * u2 + taps[1:2, :] * u1 + taps[2:3, :] * h

    a = conv_branch(0, wa_ref, ca_ref)
    b = conv_branch(1, wb_ref, cb_ref)
    o_ref[...] = (_silu(a) * b).astype(o_ref.dtype)


def _upgate_prompt(x_lo, w_up, taps, layer, nb, seq, *, tm_pref=1024, tf_pref=512):
    M, D = x_lo.shape
    F = w_up.shape[2] // 2
    tm = _tile(seq, tm_pref, _HALO)
    tf = _tile(F, tf_pref, LANE)
    nf = F // tf
    tps = seq // tm
    hb = tm // _HALO
    gated, tails = pl.pallas_call(
        functools.partial(_upgate_kernel, tps),
        out_shape=(jax.ShapeDtypeStruct((M, F), x_lo.dtype),
                   jax.ShapeDtypeStruct((M // tm, 2, FFN_K - 1, F), F32)),
        grid=(M // tm, nf),
        in_specs=[pl.BlockSpec((tm, D), lambda i, j: (i, 0)),
                  pl.BlockSpec((_HALO, D), lambda i, j: (jnp.maximum(i * hb - 1, 0), 0)),
                  pl.BlockSpec((None, D, tf), lambda i, j: (layer, 0, j)),
                  pl.BlockSpec((None, D, tf), lambda i, j: (layer, 0, nf + j)),
                  pl.BlockSpec((FFN_K, tf), lambda i, j: (0, j)),
                  pl.BlockSpec((FFN_K, tf), lambda i, j: (0, nf + j))],
        out_specs=(pl.BlockSpec((tm, tf), lambda i, j: (i, j)),
                   pl.BlockSpec((1, 2, FFN_K - 1, tf), lambda i, j: (i, 0, 0, j))),
        compiler_params=_params(("parallel", "arbitrary")),
        name="upgate_prompt",
    )(x_lo, x_lo, w_up, w_up, taps, taps)
    state = tails[tps - 1::tps].transpose(0, 2, 1, 3).reshape(nb, FFN_K - 1, 2 * F)
    return gated, state


def _sample_mix_kernel(nb, z_ref, st_ref, cos_ref, sin_ref, gam_ref, gn_ref, p0_ref, p1_ref, cw_ref,
                       ro_ref, nst_ref, co_ref, u_ref):
    D = RET_DIM
    rowi = lax.broadcasted_iota(I32, (nb, D), 0)
    di = lax.broadcasted_iota(I32, (D, D), 0)
    dj = lax.broadcasted_iota(I32, (D, D), 1)
    cos, sin = cos_ref[...], sin_ref[...]
    for h in range(RET_HEADS):
        sl = lambda c: slice((c * RET_HEADS + h) * D, (c * RET_HEADS + h + 1) * D)
        q, k, v, g = z_ref[:, sl(0)], z_ref[:, sl(1)], z_ref[:, sl(2)], z_ref[:, sl(3)]
        gam = gam_ref[:, h * D:(h + 1) * D]
        qr = q * cos + pltpu.roll(q, D // 2, axis=1) * sin
        kr = (k * cos + pltpu.roll(k, D // 2, axis=1) * sin) * (D ** -0.5)
        qf, kf, vf = (t.astype(BF16).astype(F32) for t in (qr, kr, v))
        inner = jnp.sum(qf * kf, axis=-1, keepdims=True)
        o = inner.astype(BF16).astype(F32) * vf
        cross = jnp.zeros((nb, D), F32)
        for b in range(nb):
            R = st_ref[b, h]
            cross = cross + jnp.where(rowi == b, _dot(qf.astype(BF16), R.astype(BF16)), 0.0)
            kdiag = jnp.where(di == dj, jnp.broadcast_to(kf[b:b + 1, :], (D, D)), 0.0).astype(BF16)
            vrows = jnp.broadcast_to(vf[b:b + 1, :], (D, D)).astype(BF16)
            nst_ref[b, h] = R * gam + _dot(kdiag, vrows)
        o = o + cross * gam
        mu = jnp.mean(o, axis=-1, keepdims=True)
        d = o - mu
        var = jnp.mean(d * d, axis=-1, keepdims=True)
        on = d * lax.rsqrt(var + LN_EPS) * gn_ref[:, h * D:(h + 1) * D]
        ro_ref[:, h * D:(h + 1) * D] = on * _silu(g)
    C = CONV_WIDTH
    cb, cc, ch = (z_ref[:, _Z_CONV + c * C:_Z_CONV + (c + 1) * C] for c in range(3))
    u = cc * ch
    w = cw_ref[...]
    co_ref[...] = cb * (w[0:1, :] * p0_ref[...] + w[1:2, :] * p1_ref[...] + w[2:3, :] * u)
    u_ref[...] = u


def _sample_mix(z, state, cos, sin, gam, gn, p0, p1, cw):
    nb = z.shape[0]
    D, H, C = RET_DIM, RET_HEADS, CONV_WIDTH
    return pl.pallas_call(
        functools.partial(_sample_mix_kernel, nb),
        out_shape=(jax.ShapeDtypeStruct((nb, RET_WIDTH), F32), jax.ShapeDtypeStruct((nb, H, D, D), F32),
                   jax.ShapeDtypeStruct((nb, C), F32), jax.ShapeDtypeStruct((nb, C), F32)),
        name="sample_mix",
        compiler_params=pltpu.CompilerParams(vmem_limit_bytes=_VMEM_LIMIT),
    )(z, state, cos, sin, gam, gn, p0, p1, cw)


def _sample_gate_kernel(F, h_ref, p0_ref, p1_ref, w_ref, o_ref):
    w = w_ref[...]
    y = w[0:1, :] * p0_ref[...] + w[1:2, :] * p1_ref[...] + w[2:3, :] * h_ref[...]
    o_ref[...] = _silu(y[:, :F]) * y[:, F:]


def _sample_gate(h, p0, p1, w):
    nb, F2 = h.shape
    return pl.pallas_call(
        functools.partial(_sample_gate_kernel, F2 // 2),
        out_shape=jax.ShapeDtypeStruct((nb, F2 // 2), F32),
        name="sample_gate",
    )(h, p0, p1, w)


def _sample_score_kernel(layer, topk, nb, n_pages, group, pt_ref, iq_ref, w_ref, ikn_ref, cache_ref,
                         info_ref, buf, sem, key_sc, new_sc):
    P = n_pages

    def page_copy(b, p):
        return pltpu.make_async_copy(cache_ref.at[layer, pt_ref[b, p]], buf.at[b * P + p], sem.at[b])

    for b in range(nb):
        def start(p, c, b=b):
            page_copy(b, p).start()
            return c
        lax.fori_loop(0, P, start, 0)

    half = nb // 2 if nb % 2 == 0 else nb
    for b0 in range(0, nb, half):
        tokens = range(b0, b0 + half)
        for b in tokens:
            def wait(p, c, b=b):
                page_copy(b, p).wait()
                return c
            lax.fori_loop(0, P, wait, 0)

        iqs = [iq_ref[b] for b in tokens]
        ws = [w_ref[b] for b in tokens]

        def score_group(g, carry, tokens=tokens, iqs=iqs, ws=ws):
            p0 = g * group
            for b, iq, w in zip(tokens, iqs, ws):
                ik_past = jnp.concatenate([buf[b * P + p0 + pp] for pp in range(group)], axis=1)
                d = _dot(iq, ik_past.astype(iq.dtype))
                s = _float_key(jnp.sum(jnp.maximum(d, 0.0) * w, axis=0, keepdims=True))
                for pp in range(group):
                    key_sc[b, pl.ds(p0 + pp, 1), :] = s[:, pp * PAGE_SIZE:(pp + 1) * PAGE_SIZE]
            return carry

        lax.fori_loop(0, P // group, score_group, 0)
        for b, iq, w in zip(tokens, iqs, ws):
            dn = jnp.sum(iq.astype(F32) * ikn_ref[b].astype(iq.dtype).astype(F32), axis=-1, keepdims=True)
            s_new = jnp.sum(jnp.maximum(dn, 0.0) * w, axis=0, keepdims=True)
            new_sc[b] = jnp.broadcast_to(_float_key(s_new), (8, LANE))

    keys = key_sc[...]
    key_new = new_sc[...][:, 0:1, 0:1]

    def total(x):
        part = jnp.sum(x.reshape(nb, P // 8, 8, LANE), axis=1)
        return jnp.sum(jnp.sum(part, axis=2, keepdims=True), axis=1, keepdims=True)

    def count_ge(cand):
        return total(jnp.where(keys >= cand, 1.0, 0.0)) + jnp.where(key_new >= cand, 1.0, 0.0)

    kf = float(topk)
    t0 = jnp.where(count_ge(jnp.zeros((nb, 1, 1), I32)) >= kf, 0, _INT_MIN).astype(I32)

    def bit_step(it, t):
        cand = t | lax.shift_left(jnp.int32(1), jnp.int32(30) - it)
        return jnp.where(count_ge(cand) >= kf, cand, t)

    thr = lax.fori_loop(0, 31, bit_step, t0)
    need = kf - total(jnp.where(keys > thr, 1.0, 0.0)) - jnp.where(key_new > thr, 1.0, 0.0)

    li = lax.broadcasted_iota(I32, (LANE, LANE), 0)
    lj = lax.broadcasted_iota(I32, (LANE, LANE), 1)
    upper = jnp.where(li <= lj, 1.0, 0.0).astype(BF16)
    pi = lax.broadcasted_iota(I32, (P, P), 0)
    pj = lax.broadcasted_iota(I32, (P, P), 1)
    lower_strict = jnp.where(pj < pi, 1.0, 0.0).astype(BF16)
    upper_pages = jnp.where(pi <= pj, 1.0, 0.0).astype(BF16)
    ones8 = jnp.ones((8, LANE), BF16)
    slot = lax.broadcasted_iota(I32, (topk, P), 0).astype(F32)
    page_id = lax.broadcasted_iota(I32, (topk, P), 1).astype(F32)
    out_lane = lax.broadcasted_iota(I32, (topk, LANE), 1)

    for b in range(nb):
        kb, tb, nd = keys[b], thr[b], need[b]
        eq = jnp.where(kb == tb, 1.0, 0.0)
        eq_incl = _dot(eq.astype(BF16), upper)
        eq_before = _dot(lower_strict, jnp.broadcast_to(eq_incl[:, LANE - 1:LANE], eq.shape).astype(BF16))
        eq_rank = eq_before + eq_incl - eq
        sel = jnp.where(kb > tb, 1.0, eq * jnp.where(eq_rank < nd, 1.0, 0.0))
        sel_lo = sel.astype(BF16)
        within = _dot(sel_lo, upper)
        cnt_row = _dot_nt(ones8, sel_lo)
        incl_row = _dot(cnt_row.astype(BF16), upper_pages)[0:1, :]
        n_sel = incl_row[:, P - 1:P]
        done = jnp.where(incl_row <= slot[:, 0:1], 1.0, 0.0)
        page = jnp.sum(done, axis=1, keepdims=True)
        before = jnp.sum(done * cnt_row[0:1, :], axis=1, keepdims=True)
        r = slot[:, 0:1] - before
        pick = jnp.where(page_id == page, 1.0, 0.0).astype(BF16)
        within_slot = _dot(pick, within.astype(BF16))
        off = jnp.sum(jnp.where(within_slot <= r, 1.0, 0.0), axis=1, keepdims=True)
        filled = slot[:, 0:1] < n_sel
        info = jnp.where(out_lane == 0, off, jnp.where(out_lane == 1, page, 1.0))
        info = jnp.where(filled & (out_lane < 3), info, 0.0)
        info_ref[b] = info.astype(I32)


def _sample_score(page_table, iq, w, ikn, cache_idx_k, layer, topk):
    nb, n_pages = page_table.shape
    group = _tile(n_pages, 8, 1)
    whole = lambda i, pt: (0, 0, 0)
    grid_spec = pltpu.PrefetchScalarGridSpec(
        num_scalar_prefetch=1,
        grid=(1,),
        in_specs=[pl.BlockSpec((nb, IDX_HEADS, IDX_DIM), whole),
                  pl.BlockSpec((nb, IDX_HEADS, 1), whole),
                  pl.BlockSpec((nb, 1, IDX_DIM), whole),
                  pl.BlockSpec(memory_space=pl.ANY)],
        out_specs=pl.BlockSpec((nb, topk, LANE), whole),
        scratch_shapes=[pltpu.VMEM((nb * n_pages, IDX_DIM, PAGE_SIZE), F32),
                        pltpu.SemaphoreType.DMA((nb,)),
                        pltpu.VMEM((nb, n_pages, PAGE_SIZE), I32),
                        pltpu.VMEM((nb, 8, LANE), I32)],
    )
    return pl.pallas_call(
        functools.partial(_sample_score_kernel, layer, topk, nb, n_pages, group),
        out_shape=jax.ShapeDtypeStruct((nb, topk, LANE), I32),
        grid_spec=grid_spec,
        compiler_params=_params(("arbitrary",)),
        name="sample_score",
    )(page_table, iq, w, ikn, cache_idx_k)


def _sample_attn_kernel(layer, topk, pt_ref, page_ref, off_ref, q_ref, kn_ref, vn_ref, inpast_ref,
                        ck_ref, cv_ref, o_ref, kbuf, vbuf, sem):
    b = pl.program_id(0)
    nb = pl.num_programs(0)
    slot = b % 2

    def copies(bb, sl, j):
        src = (layer, pt_ref[bb, page_ref[bb, j]], off_ref[bb, j])
        return (pltpu.make_async_copy(ck_ref.at[src], kbuf.at[sl, j], sem.at[0, sl]),
                pltpu.make_async_copy(cv_ref.at[src], vbuf.at[sl, j], sem.at[1, sl]))

    def start_rows(bb, sl):
        def start(j, c):
            ck, cv = copies(bb, sl, j)
            ck.start()
            cv.start()
            return c
        lax.fori_loop(0, topk, start, 0)

    @pl.when(b == 0)
    def _():
        start_rows(b, slot)

    @pl.when(b + 1 < nb)
    def _():
        start_rows(b + 1, 1 - slot)

    def wait(j, c):
        ck, cv = copies(b, slot, j)
        ck.wait()
        cv.wait()
        return c

    lax.fori_loop(0, topk, wait, 0)

    inpast = inpast_ref[0] > 0
    ks = jnp.where(inpast, kbuf[slot], kn_ref[...])
    vs = jnp.where(inpast, vbuf[slot], vn_ref[...])
    q = q_ref[...].astype(BF16).astype(F32)
    s = jnp.sum(q * ks.astype(BF16).astype(F32), axis=-1, keepdims=True) * ATT_SCALE
    m = jnp.max(s, axis=0, keepdims=True)
    e = jnp.exp(s - m)
    p = e / jnp.sum(e, axis=0, keepdims=True)
    o_ref[...] = jnp.sum(p.astype(BF16).astype(F32) * vs.astype(BF16).astype(F32), axis=0, keepdims=True)


def _sample_attn(page_table, page, off, q, kn, vn, inpast, cache_k, cache_v, layer, topk):
    nb = q.shape[0]
    H, D = ATT_HEADS, ATT_DIM
    per_b = lambda b, pt, pg, of: (b, 0, 0)
    grid_spec = pltpu.PrefetchScalarGridSpec(
        num_scalar_prefetch=3,
        grid=(nb,),
        in_specs=[pl.BlockSpec((1, H, D), per_b), pl.BlockSpec((1, H, D), per_b),
                  pl.BlockSpec((1, H, D), per_b),
                  pl.BlockSpec((1, topk, 1, 1), lambda b, pt, pg, of: (b, 0, 0, 0)),
                  pl.BlockSpec(memory_space=pl.ANY), pl.BlockSpec(memory_space=pl.ANY)],
        out_specs=pl.BlockSpec((1, H, D), per_b),
        scratch_shapes=[pltpu.VMEM((2, topk, H, D), F32), pltpu.VMEM((2, topk, H, D), F32),
                        pltpu.SemaphoreType.DMA((2, 2))],
    )
    return pl.pallas_call(
        functools.partial(_sample_attn_kernel, layer, topk),
        out_shape=jax.ShapeDtypeStruct((nb, H, D), F32),
        grid_spec=grid_spec,
        compiler_params=_params(("arbitrary",)),
        name="sample_attn",
    )(page_table, page, off, q, kn, vn, inpast, cache_k, cache_v)


def kernel(x_prompt, x_sample, cache_k, cache_v, cache_idx_k, state_ret, state_conv, state_ffn,
           page_table, w_in, ret_gn_g, idx_kn_g, idx_kn_b, conv_w, w_out, ln1_g, ln1_b,
           w_up, ffn_conv_w, w_down, ln2_g, ln2_b):
    B, S, D = x_prompt.shape
    DB, T, _ = x_sample.shape
    assert T == 1 and S % PAGE_SIZE == 0 and w_in.shape[-1] == _D_IN
    depth = w_in.shape[0]
    n_pages = page_table.shape[1]
    past = n_pages * PAGE_SIZE
    F2 = w_up.shape[-1]
    alpha = (2 * depth) ** 0.25
    topk_s = min(TOPK_MAX, (past + T) // 4)

    pos_p = jnp.arange(S, dtype=jnp.int32)
    pos_s = past + jnp.arange(T, dtype=jnp.int32)
    att_tabs_p = _rope_tables(pos_p, ROT_DIM, ROPE_THETA, ATT_DIM)
    idx_tabs_p = _rope_tables(pos_p, IDX_ROT_DIM, ROPE_THETA, IDX_DIM)
    rc, ru, rd = _rope_tables(pos_p, RET_DIM, RET_THETA, RET_DIM)
    ret_cos_p, ret_sin_p = rc, ru + rd
    bcast = lambda t: jnp.broadcast_to(t, (DB, LANE))
    att_tabs_s = tuple(bcast(t) for t in _rope_tables(pos_s, ROT_DIM, ROPE_THETA, ATT_DIM))
    idx_tabs_s = tuple(bcast(t) for t in _rope_tables(pos_s, IDX_ROT_DIM, ROPE_THETA, IDX_DIM))
    rc, ru, rd = _rope_tables(pos_s, RET_DIM, RET_THETA, RET_DIM)
    ret_cos_s, ret_sin_s = bcast(rc), bcast(ru + rd)
    gamma = 1.0 - 2.0 ** (-5.0 - jnp.arange(RET_HEADS, dtype=F32))
    gamma = jnp.exp(jnp.log(gamma))
    gam_row = jnp.repeat(gamma, RET_DIM)[None, :]

    cache_idx_t = jnp.swapaxes(cache_idx_k, 2, 3)
    xp = x_prompt.reshape(B * S, D)
    xp_lo = xp.astype(BF16)
    xs = x_sample.reshape(DB, D)

    outs_p = [[] for _ in range(6)]
    outs_s = [[] for _ in range(6)]
    row2 = lambda v: v.reshape(1, -1)
    pad_small = lambda v: jnp.pad(v, (0, _SMALL_W - v.shape[0])).reshape(1, _SMALL_W)

    w_main = jnp.concatenate([w_in[:, :, :_O_IK], w_in[:, :, _O_CONV:]], axis=2).astype(BF16)
    w_small = jnp.pad(w_in[:, :, _O_IK:_O_CONV],
                      ((0, 0), (0, 0), (0, _SMALL_W - (_O_CONV - _O_IK)))).astype(BF16)
    w_out_lo, w_up_lo, w_down_lo = w_out.astype(BF16), w_up.astype(BF16), w_down.astype(BF16)

    for l in range(depth):
        gn = row2(ret_gn_g[l])
        kg, kb_ = pad_small(idx_kn_g[l]), pad_small(idx_kn_b[l])
        g1, b1, g2, b2 = row2(ln1_g[l]), row2(ln1_b[l]), row2(ln2_g[l]), row2(ln2_b[l])

        z, zs = _inproj(xp_lo, w_main, w_small, l)
        ro, p_ret = _retention_prompt(z, ret_cos_p, ret_sin_p, gn, B, S)
        qb, kb, vT, kf, iqb, ikb, smT = _prep(z, zs, att_tabs_p, idx_tabs_p, kg, kb_, S, transposed=True)
        ao = _dsa_prompt(iqb, ikb, smT, qb, kb, vT, B, S)
        co, p_conv = _conv_prompt(z, conv_w[l], B, S)
        x1, x1_lo = _outproj_ln(ro, ao, co, w_out_lo, l, xp, g1, b1, alpha)
        gt, p_ffn = _upgate_prompt(x1_lo, w_up_lo, ffn_conv_w[l], l, B, S)
        xp, xp_lo = _down_ln(gt, w_down_lo, l, x1, g2, b2, alpha, tk_pref=1408)
        outs_p[0].append(kf.reshape(B, S, ATT_HEADS, ATT_DIM))
        outs_p[1].append(z[:, _O_ATT + 2 * ATT_WIDTH:_O_ATT + 3 * ATT_WIDTH].reshape(B, S, ATT_HEADS, ATT_DIM))
        outs_p[2].append(smT[:IDX_DIM, :].T.reshape(B, S, IDX_DIM))
        outs_p[3].append(p_ret)
        outs_p[4].append(p_conv)
        outs_p[5].append(p_ffn)

        z, zs = _inproj(xs, w_main, w_small, l)
        ro, s_ret, co, u_new = _sample_mix(z, state_ret[l], ret_cos_s, ret_sin_s, gam_row, gn,
                                           state_conv[l, :, 0], state_conv[l, :, 1], conv_w[l])
        q_s, _, _, k_s, iq_s, _, sm_s = _prep(z, zs, att_tabs_s, idx_tabs_s, kg, kb_, DB,
                                              transposed=False, lowp_dtype=F32)
        v_s = z[:, _O_ATT + 2 * ATT_WIDTH:_O_ATT + 3 * ATT_WIDTH]
        ik_s = sm_s[:, :IDX_DIM]
        iw_s = sm_s[:, IDX_DIM:IDX_DIM + IDX_HEADS]
        info = _sample_score(page_table, iq_s.reshape(DB, IDX_HEADS, IDX_DIM).astype(BF16),
                             iw_s.reshape(DB, IDX_HEADS, 1), ik_s.reshape(DB, 1, IDX_DIM),
                             cache_idx_t, l, topk_s)
        hd = lambda t: t.reshape(DB, ATT_HEADS, ATT_DIM)
        ao = _sample_attn(page_table, info[:, :, 1], info[:, :, 0], hd(q_s), hd(k_s), hd(v_s),
                          info[:, :, 2].reshape(DB, topk_s, 1, 1), cache_k, cache_v, l, topk_s)
        x1, _ = _outproj_ln(ro, ao.reshape(DB, ATT_WIDTH), co, w_out_lo, l, xs, g1, b1, alpha,
                            lowp_dtype=F32)
        h = _matmul(x1, w_up_lo, l, name="up_s")
        gt = _sample_gate(h, state_ffn[l, :, 0], state_ffn[l, :, 1], ffn_conv_w[l])
        xs_new, _ = _down_ln(gt, w_down_lo, l, x1, g2, b2, alpha, lowp_dtype=F32)
        outs_s[0].append(k_s.reshape(DB, T, ATT_HEADS, ATT_DIM))
        outs_s[1].append(v_s.reshape(DB, T, ATT_HEADS, ATT_DIM))
        outs_s[2].append(ik_s.reshape(DB, T, IDX_DIM))
        outs_s[3].append(s_ret)
        outs_s[4].append(jnp.stack([state_conv[l, :, 1], u_new], axis=1))
        outs_s[5].append(jnp.stack([state_ffn[l, :, 1], h], axis=1))
        xs = xs_new

    n_pp = S // PAGE_SIZE
    st = lambda xs_: jnp.stack(xs_)
    return (xp.reshape(B, S, D), xs.reshape(DB, T, D),
            st(outs_p[0]).reshape(depth, B, n_pp, PAGE_SIZE, ATT_HEADS, ATT_DIM),
            st(outs_p[1]).reshape(depth, B, n_pp, PAGE_SIZE, ATT_HEADS, ATT_DIM),
            st(outs_p[2]).reshape(depth, B, n_pp, PAGE_SIZE, IDX_DIM),
            st(outs_p[3]), st(outs_p[4]), st(outs_p[5]),
            st(outs_s[0]), st(outs_s[1]), st(outs_s[2]), st(outs_s[3]), st(outs_s[4]), st(outs_s[5]))
```

```python
import functools

import numpy as np
import jax
import jax.numpy as jnp
from jax import lax
from jax.experimental import pallas as pl
from jax.experimental.pallas import tpu as pltpu

F32 = jnp.float32
BF16 = jnp.bfloat16
I32 = jnp.int32

PAGE_SIZE = 128
RET_HEADS, RET_DIM, RET_CHUNK, RET_THETA = 4, 128, 128, 10000.0
RET_WIDTH = RET_HEADS * RET_DIM
ATT_HEADS, ATT_DIM = 8, 128
ATT_WIDTH = ATT_HEADS * ATT_DIM
ROT_DIM = ATT_DIM // 4
ROPE_THETA = 500000.0
IDX_HEADS, IDX_DIM = 16, 64
IDX_WIDTH = IDX_HEADS * IDX_DIM
IDX_ROT_DIM = IDX_DIM // 4
TOPK_MAX = 256
ATT_SCALE = ATT_DIM ** -0.5
IDX_W_SCALE = (IDX_HEADS * IDX_DIM) ** -0.5
CONV_WIDTH, CONV_K = 512, 3
FFN_K = 3
LN_EPS = 1e-5

_O_RET = 0
_O_ATT = 4 * RET_WIDTH
_O_IQ = _O_ATT + 3 * ATT_WIDTH
_O_IK = _O_IQ + IDX_WIDTH
_O_IW = _O_IK + IDX_DIM
_O_CONV = _O_IW + IDX_HEADS
_D_IN = _O_CONV + 3 * CONV_WIDTH
_Z_CONV = _O_IK
_Z_MAIN = _Z_CONV + 3 * CONV_WIDTH
LANE = 128
_SMALL_W = LANE

_NEG_BIG = -1e30
_INT_MIN = -2 ** 31
_KEY_NEG_INF = int(np.int32(np.uint32(0xFF800000 ^ 0x7FFFFFFF)))

_VMEM_LIMIT = 56 * 1024 * 1024
_PROJ_TN = 512


def _params(sem):
    return pltpu.CompilerParams(dimension_semantics=sem, vmem_limit_bytes=_VMEM_LIMIT)


def _tile(dim, pref, mult):
    t = min(pref, dim)
    t -= t % mult
    while t >= mult:
        if dim % t == 0:
            return t
        t -= mult
    return dim


def _float_key(s):
    b = lax.bitcast_convert_type(s, I32)
    return b ^ ((b >> 31) & jnp.int32(0x7FFFFFFF))


def _layer_norm_rows(r, g, b):
    mu = jnp.mean(r, axis=-1, keepdims=True)
    d = r - mu
    var = jnp.mean(d * d, axis=-1, keepdims=True)
    return d * lax.rsqrt(var + LN_EPS) * g + b


def _silu(x):
    return x * (1.0 / (1.0 + jnp.exp(-x)))


def _dot(a, b):
    return jnp.dot(a, b, preferred_element_type=F32)


def _dot_nt(a, b):
    return lax.dot_general(a, b, (((1,), (1,)), ((), ())), preferred_element_type=F32)


def _mm_kernel(x_ref, w_ref, o_ref):
    o_ref[...] = _dot(x_ref[...].astype(BF16), w_ref[...]).astype(o_ref.dtype)


def _col_tiles(w, tn):
    depth, K, N = w.shape
    return w.reshape(depth, K, N // tn, tn).transpose(0, 2, 1, 3)


def _matmul(x, w_tiles, layer, *, tm_pref=1024, out_dtype=F32, name="mm"):
    M, K = x.shape
    nt, tn = w_tiles.shape[1], w_tiles.shape[3]
    N = nt * tn
    tm = _tile(M, tm_pref, 8)
    return pl.pallas_call(
        _mm_kernel,
        out_shape=jax.ShapeDtypeStruct((M, N), out_dtype),
        grid=(M // tm, nt),
        in_specs=[pl.BlockSpec((tm, K), lambda i, j: (i, 0)),
                  pl.BlockSpec((None, None, K, tn), lambda i, j: (layer, j, 0, 0))],
        out_specs=pl.BlockSpec((tm, tn), lambda i, j: (i, j)),
        compiler_params=_params(("parallel", "arbitrary")),
        name=name,
    )(x, w_tiles)


def _inproj_kernel(x_ref, w_ref, ws_ref, o_ref, os_ref):
    x = x_ref[...].astype(BF16)
    o_ref[...] = _dot(x, w_ref[...])

    @pl.when(pl.program_id(1) == 0)
    def _():
        os_ref[...] = _dot(x, ws_ref[...])


def _inproj(x, w_tiles, w_small, layer, *, tm_pref=2048):
    M, K = x.shape
    nt, tn = w_tiles.shape[1], w_tiles.shape[3]
    N = nt * tn
    tm = _tile(M, tm_pref, 8)
    return pl.pallas_call(
        _inproj_kernel,
        out_shape=(jax.ShapeDtypeStruct((M, N), F32), jax.ShapeDtypeStruct((M, _SMALL_W), F32)),
        grid=(M // tm, nt),
        in_specs=[pl.BlockSpec((tm, K), lambda i, j: (i, 0)),
                  pl.BlockSpec((None, None, K, tn), lambda i, j: (layer, j, 0, 0)),
                  pl.BlockSpec((None, K, _SMALL_W), lambda i, j: (layer, 0, 0))],
        out_specs=(pl.BlockSpec((tm, tn), lambda i, j: (i, j)),
                   pl.BlockSpec((tm, _SMALL_W), lambda i, j: (i, 0))),
        compiler_params=_params(("parallel", "arbitrary")),
        name="inproj",
    )(x, w_tiles, w_small)


def _outproj_ln_kernel(alpha, ro_ref, ao_ref, co_ref, w_ref, x_ref, g_ref, b_ref, o_ref, ob_ref):
    y = _dot(ro_ref[...].astype(BF16), w_ref[0:RET_WIDTH, :])
    y += _dot(ao_ref[...].astype(BF16), w_ref[RET_WIDTH:RET_WIDTH + ATT_WIDTH, :])
    y += _dot(co_ref[...].astype(BF16), w_ref[RET_WIDTH + ATT_WIDTH:, :])
    out = _layer_norm_rows(alpha * x_ref[...] + y, g_ref[...], b_ref[...])
    o_ref[...] = out
    ob_ref[...] = out.astype(ob_ref.dtype)


def _outproj_ln(ro, ao, co, w_out, layer, x, g, b, alpha, *, tm_pref=512, lowp_dtype=BF16):
    M, D = x.shape
    tm = _tile(M, tm_pref, 8)
    row = lambda i: (i, 0)
    fixed = lambda i: (0, 0)
    return pl.pallas_call(
        functools.partial(_outproj_ln_kernel, alpha),
        out_shape=(jax.ShapeDtypeStruct((M, D), F32), jax.ShapeDtypeStruct((M, D), lowp_dtype)),
        grid=(M // tm,),
        in_specs=[pl.BlockSpec((tm, RET_WIDTH), row), pl.BlockSpec((tm, ATT_WIDTH), row),
                  pl.BlockSpec((tm, CONV_WIDTH), row),
                  pl.BlockSpec((None,) + w_out.shape[1:], lambda i: (layer, 0, 0)),
                  pl.BlockSpec((tm, D), row), pl.BlockSpec((1, D), fixed), pl.BlockSpec((1, D), fixed)],
        out_specs=(pl.BlockSpec((tm, D), row), pl.BlockSpec((tm, D), row)),
        compiler_params=_params(("parallel",)),
        name="outproj_ln",
    )(ro, ao, co, w_out, x, g, b)


def _down_ln_kernel(alpha, nk, gt_ref, w_ref, x_ref, g_ref, b_ref, o_ref, ob_ref, acc_ref):
    k = pl.program_id(1)

    @pl.when(k == 0)
    def _():
        acc_ref[...] = jnp.zeros_like(acc_ref)

    acc_ref[...] += _dot(gt_ref[...].astype(BF16), w_ref[...])

    @pl.when(k == nk - 1)
    def _():
        out = _layer_norm_rows(alpha * x_ref[...] + acc_ref[...], g_ref[...], b_ref[...])
        o_ref[...] = out
        ob_ref[...] = out.astype(ob_ref.dtype)


def _down_ln(gt, w_down, layer, x, g, b, alpha, *, tm_pref=512, tk_pref=512, lowp_dtype=BF16):
    M, D = x.shape
    K = gt.shape[1]
    tm = _tile(M, tm_pref, 8)
    tk = _tile(K, tk_pref, LANE)
    nk = K // tk
    return pl.pallas_call(
        functools.partial(_down_ln_kernel, alpha, nk),
        out_shape=(jax.ShapeDtypeStruct((M, D), F32), jax.ShapeDtypeStruct((M, D), lowp_dtype)),
        grid=(M // tm, nk),
        in_specs=[pl.BlockSpec((tm, tk), lambda i, k: (i, k)),
                  pl.BlockSpec((None, tk, D), lambda i, k: (layer, k, 0)),
                  pl.BlockSpec((tm, D), lambda i, k: (i, 0)),
                  pl.BlockSpec((1, D), lambda i, k: (0, 0)),
                  pl.BlockSpec((1, D), lambda i, k: (0, 0))],
        out_specs=(pl.BlockSpec((tm, D), lambda i, k: (i, 0)),
                   pl.BlockSpec((tm, D), lambda i, k: (i, 0))),
        scratch_shapes=[pltpu.VMEM((tm, D), F32)],
        compiler_params=_params(("parallel", "arbitrary")),
        name="down_ln",
    )(gt, w_down, x, g, b)


def _rope_tables(pos, rot_dim, theta, period):
    half = rot_dim // 2
    inv = theta ** (-jnp.arange(half, dtype=F32) / half)
    ang = pos.astype(F32)[:, None] * inv[None, :]
    cos, sin = jnp.cos(ang), jnp.sin(ang)
    L = pos.shape[0]
    rest = period - rot_dim
    c = jnp.concatenate([cos, cos, jnp.ones((L, rest), F32)], axis=1)
    s_up = jnp.concatenate([-sin, jnp.zeros((L, half + rest), F32)], axis=1)
    s_dn = jnp.concatenate([jnp.zeros((L, half), F32), sin, jnp.zeros((L, rest), F32)], axis=1)
    rep = LANE // period
    return tuple(jnp.tile(t, (1, rep)) for t in (c, s_up, s_dn))


def _rope128(x, c, s_up, s_dn, half):
    if 2 * half == LANE:
        return x * c + pltpu.roll(x, half, axis=1) * (s_up + s_dn)
    return x * c + pltpu.roll(x, LANE - half, axis=1) * s_up + pltpu.roll(x, half, axis=1) * s_dn


def _prep_kernel(transposed, aq_ref, ak_ref, av_ref, iq_ref, sm_ref, ac_ref, au_ref, ad_ref,
                 ic_ref, iu_ref, id_ref, kg_ref, kb_ref,
                 q_o, k_o, v_o, kf_o, iq_o, ik_o, sm_o):
    ac, au, ad = ac_ref[...], au_ref[...], ad_ref[...]
    ic, iu, idn = ic_ref[...], iu_ref[...], id_ref[...]
    for h in range(ATT_HEADS):
        sl = slice(h * ATT_DIM, (h + 1) * ATT_DIM)
        q_o[:, sl] = _rope128(aq_ref[:, sl], ac, au, ad, ROT_DIM // 2).astype(q_o.dtype)
        kr = _rope128(ak_ref[:, sl], ac, au, ad, ROT_DIM // 2)
        kf_o[:, sl] = kr
        k_o[:, sl] = kr.astype(k_o.dtype)
        if transposed:
            v_o[sl, :] = av_ref[:, sl].T.astype(v_o.dtype)
    if not transposed:
        v_o[...] = av_ref[...].astype(v_o.dtype)
    for c in range(IDX_WIDTH // LANE):
        sl = slice(c * LANE, (c + 1) * LANE)
        iq_o[:, sl] = _rope128(iq_ref[:, sl], ic, iu, idn, IDX_ROT_DIM // 2).astype(iq_o.dtype)
    sm = sm_ref[...]
    lane = lax.broadcasted_iota(I32, sm.shape, 1)
    is_k = lane < IDX_DIM
    mu = jnp.sum(jnp.where(is_k, sm, 0.0), axis=-1, keepdims=True) * (1.0 / IDX_DIM)
    d = jnp.where(is_k, sm - mu, 0.0)
    var = jnp.sum(d * d, axis=-1, keepdims=True) * (1.0 / IDX_DIM)
    n = d * lax.rsqrt(var + LN_EPS) * kg_ref[...] + kb_ref[...]
    roped = _rope128(n, ic, iu, idn, IDX_ROT_DIM // 2)
    out = jnp.where(is_k, roped, jnp.where(lane < IDX_DIM + IDX_HEADS, sm * IDX_W_SCALE, 0.0))
    sm_o[...] = out.T if transposed else out
    ik_o[...] = out.astype(ik_o.dtype)


def _prep(z, zs, att_tabs, idx_tabs, kg, kb, seq, *, transposed, tm_pref=256, lowp_dtype=BF16):
    M = z.shape[0]
    tm = _tile(seq, tm_pref, 8)
    nt = seq // tm
    col = lambda c: (lambda i: (i, c))
    tab = lambda i: (i % nt, 0)
    fixed = lambda i: (0, 0)
    W = ATT_WIDTH
    big = lambda dt: jax.ShapeDtypeStruct((M, W), dt)
    rows, small = pl.BlockSpec((tm, W), col(0)), pl.BlockSpec((tm, LANE), col(0))
    if transposed:
        v_shape, v_spec = jax.ShapeDtypeStruct((W, M), lowp_dtype), pl.BlockSpec((W, tm), lambda i: (0, i))
        sm_shape, sm_spec = jax.ShapeDtypeStruct((LANE, M), F32), pl.BlockSpec((LANE, tm), lambda i: (0, i))
    else:
        v_shape, v_spec = big(lowp_dtype), rows
        sm_shape, sm_spec = jax.ShapeDtypeStruct((M, LANE), F32), small
    return pl.pallas_call(
        functools.partial(_prep_kernel, transposed),
        out_shape=(big(lowp_dtype), big(lowp_dtype), v_shape, big(F32), big(lowp_dtype),
                   jax.ShapeDtypeStruct((M, LANE), lowp_dtype), sm_shape),
        grid=(M // tm,),
        in_specs=[pl.BlockSpec((tm, W), col(_O_ATT // W)), pl.BlockSpec((tm, W), col(_O_ATT // W + 1)),
                  pl.BlockSpec((tm, W), col(_O_ATT // W + 2)), pl.BlockSpec((tm, W), col(_O_IQ // W)),
                  pl.BlockSpec((tm, LANE), col(0))]
        + [pl.BlockSpec((tm, LANE), tab)] * 6
        + [pl.BlockSpec((1, LANE), fixed)] * 2,
        out_specs=(rows, rows, v_spec, rows, rows, small, sm_spec),
        compiler_params=_params(("parallel",)),
        name="prep",
    )(z, z, z, z, zs, *att_tabs, *idx_tabs, kg, kb)


def _fold8(x, op):
    return op(x.reshape(x.shape[0] // 8, 8, x.shape[1]), axis=0)


def _dsa_kernel(topk, tq, iq_ref, ik_ref, smT_ref, q_ref, k_ref, vT_ref, o_ref,
                key_sc, bias_sc, s_sc, acc_sc):
    tk = tq
    i = pl.program_id(1)
    nkb = i + 1
    qpos = i * tq + lax.broadcasted_iota(I32, (tk, tq), 1)

    def score_block(kb, carry):
        k0 = pl.multiple_of(kb * tk, tk)
        ikb = ik_ref[pl.ds(k0, tk), 0:IDX_DIM]
        acc = jnp.zeros((tk, tq), F32)
        for h in range(IDX_HEADS):
            d = _dot_nt(ikb, iq_ref[:, h * IDX_DIM:(h + 1) * IDX_DIM])
            acc = acc + jnp.maximum(d, 0.0) * smT_ref[IDX_DIM + h:IDX_DIM + h + 1, :]
        kpos = k0 + lax.broadcasted_iota(I32, (tk, tq), 0)
        key_sc[kb] = _float_key(jnp.where(kpos <= qpos, acc, -jnp.inf))
        return carry

    lax.fori_loop(0, nkb, score_block, 0)

    def count_ge(cand):
        def body(kb, part):
            return part + _fold8(jnp.where(key_sc[kb] >= cand, 1.0, 0.0), jnp.sum)
        part = lax.fori_loop(0, nkb, body, jnp.zeros((8, tq), F32))
        return jnp.sum(part, axis=0, keepdims=True)

    kf = float(topk)
    t0 = jnp.where(count_ge(jnp.zeros((1, tq), I32)) >= kf, 0, _INT_MIN).astype(I32)

    def bit_step(it, t):
        cand = t | lax.shift_left(jnp.int32(1), jnp.int32(30) - it)
        return jnp.where(count_ge(cand) >= kf, cand, t)

    thr = lax.fori_loop(0, 31, bit_step, t0)
    thr = jnp.maximum(thr, _KEY_NEG_INF + 1)
    has_ties = jnp.max(count_ge(thr)) > kf

    @pl.when(jnp.logical_not(has_ties))
    def _():
        def bias_block(kb, carry):
            bias_sc[kb] = jnp.where(key_sc[kb] >= thr, 0.0, _NEG_BIG)
            return carry

        lax.fori_loop(0, nkb, bias_block, 0)

    @pl.when(has_ties)
    def _():
        need = kf - count_ge(thr + 1)
        ki = lax.broadcasted_iota(I32, (tk, tk), 0)
        kj = lax.broadcasted_iota(I32, (tk, tk), 1)
        lower_strict = jnp.where(kj < ki, 1.0, 0.0).astype(BF16)

        def bias_block(kb, before):
            keys = key_sc[kb]
            eq = jnp.where(keys == thr, 1.0, 0.0)
            rank = _dot(lower_strict, eq.astype(BF16)) + before
            take = jnp.where(keys > thr, 1.0, eq * jnp.where(rank < need, 1.0, 0.0))
            bias_sc[kb] = jnp.where(take > 0.0, 0.0, _NEG_BIG)
            return before + jnp.sum(_fold8(eq, jnp.sum), axis=0, keepdims=True)

        lax.fori_loop(0, nkb, bias_block, jnp.zeros((1, tq), F32))

    heads = [slice(h * ATT_DIM, (h + 1) * ATT_DIM) for h in range(ATT_HEADS)]

    def scores_pass(kb, mparts):
        k0 = pl.multiple_of(kb * tk, tk)
        bias = bias_sc[kb]
        new = []
        for h, sl in enumerate(heads):
            s = _dot_nt(k_ref[pl.ds(k0, tk), sl], q_ref[:, sl]) + bias
            s_sc[h, kb] = s
            new.append(jnp.maximum(mparts[h], _fold8(s, jnp.max)))
        return tuple(new)

    mparts = lax.fori_loop(0, nkb, scores_pass,
                           tuple(jnp.full((8, tq), _NEG_BIG, F32) for _ in heads))
    ms = [jnp.max(mp, axis=0, keepdims=True) for mp in mparts]
    acc_sc[...] = jnp.zeros(acc_sc.shape, F32)
    c = ATT_SCALE * float(np.log2(np.e))

    def values_pass(kb, lparts):
        k0 = pl.multiple_of(kb * tk, tk)
        new = []
        for h, sl in enumerate(heads):
            p = jnp.exp2((s_sc[h, kb] - ms[h]) * c)
            new.append(lparts[h] + _fold8(p, jnp.sum))
            acc_sc[sl, :] += _dot(vT_ref[sl, pl.ds(k0, tk)], p.astype(vT_ref.dtype))
        return tuple(new)

    lparts = lax.fori_loop(0, nkb, values_pass, tuple(jnp.zeros((8, tq), F32) for _ in heads))
    for h, sl in enumerate(heads):
        l = jnp.sum(lparts[h], axis=0, keepdims=True)
        o_ref[:, sl] = (acc_sc[sl, :] / l).T.astype(o_ref.dtype)


def _dsa_prompt(iqb, ikb, smT, qb, kb, vT, nb, seq, *, tq_pref=256, out_dtype=BF16):
    topk = min(TOPK_MAX, seq // 4)
    tq = _tile(seq, tq_pref, LANE)
    nq = seq // tq
    W = ATT_WIDTH
    qblk = lambda b, i: (b * nq + i, 0)
    qcols = lambda b, i: (0, b * nq + i)
    seqblk = lambda b, i: (b, 0)
    return pl.pallas_call(
        functools.partial(_dsa_kernel, topk, tq),
        out_shape=jax.ShapeDtypeStruct((nb * seq, W), out_dtype),
        grid=(nb, nq),
        in_specs=[pl.BlockSpec((tq, IDX_WIDTH), qblk), pl.BlockSpec((seq, LANE), seqblk),
                  pl.BlockSpec((LANE, tq), qcols), pl.BlockSpec((tq, W), qblk),
                  pl.BlockSpec((seq, W), seqblk), pl.BlockSpec((W, seq), lambda b, i: (0, b))],
        out_specs=pl.BlockSpec((tq, W), qblk),
        scratch_shapes=[pltpu.VMEM((nq, tq, tq), I32), pltpu.VMEM((nq, tq, tq), F32),
                        pltpu.VMEM((ATT_HEADS, nq, tq, tq), F32), pltpu.VMEM((W, tq), F32)],
        compiler_params=_params(("parallel", "arbitrary")),
        name="dsa_prompt",
    )(iqb, ikb, smT, qb, kb, vT)


def _ret_tables(chunk):
    lg = jnp.log(1.0 - 2.0 ** (-5.0 - jnp.arange(RET_HEADS, dtype=F32)))
    i = jnp.arange(chunk, dtype=F32)
    diff = i[:, None] - i[None, :]
    decay = jnp.where(diff >= 0, jnp.exp(lg[:, None, None] * jnp.maximum(diff, 0.0)), 0.0)
    xi = jnp.exp(lg[:, None] * (i + 1.0))[:, :, None]
    zeta = jnp.exp(lg[:, None] * (chunk - 1.0 - i))[:, :, None]
    g_c = jnp.exp(lg * chunk)[:, None, None]
    return decay, xi, zeta, g_c


def _ret_kernel(nchunk, chunk, q_ref, k_ref, v_ref, g_ref, cos_ref, sin_ref, dec_ref, xi_ref,
                zeta_ref, gc_ref, gn_ref, o_ref, st_ref):
    D = RET_DIM
    st_ref[...] = jnp.zeros(st_ref.shape, F32)

    def step(c, carry):
        r0 = pl.multiple_of(c * chunk, chunk)
        rows = pl.ds(r0, chunk)
        cos, sin = cos_ref[rows, :], sin_ref[rows, :]
        for h in range(RET_HEADS):
            sl = slice(h * D, (h + 1) * D)
            q = q_ref[rows, sl]
            k = k_ref[rows, sl]
            qr = q * cos + pltpu.roll(q, D // 2, axis=1) * sin
            kr = (k * cos + pltpu.roll(k, D // 2, axis=1) * sin) * (D ** -0.5)
            qb, vb = qr.astype(BF16), v_ref[rows, sl].astype(BF16)
            R = st_ref[0, h]
            inner = _dot_nt(qb, kr.astype(BF16)) * dec_ref[h]
            o = _dot(inner.astype(BF16), vb) + _dot(qb, R.astype(BF16)) * xi_ref[h]
            kz = (kr * zeta_ref[h]).astype(BF16)
            st_ref[0, h] = R * gc_ref[h] + lax.dot_general(kz, vb, (((0,), (0,)), ((), ())),
                                                           preferred_element_type=F32)
            mu = jnp.mean(o, axis=-1, keepdims=True)
            d = o - mu
            var = jnp.mean(d * d, axis=-1, keepdims=True)
            on = d * lax.rsqrt(var + LN_EPS) * gn_ref[:, sl]
            o_ref[rows, sl] = (on * _silu(g_ref[rows, sl])).astype(o_ref.dtype)
        return carry

    lax.fori_loop(0, nchunk, step, 0)


def _retention_prompt(z, cos, sin, gn, nb, seq, *, out_dtype=BF16):
    chunk = RET_CHUNK if seq % RET_CHUNK == 0 else seq
    nchunk = seq // chunk
    decay, xi, zeta, g_c = _ret_tables(chunk)
    D, H, W = RET_DIM, RET_HEADS, RET_WIDTH
    col = lambda c: (lambda b: (b, c))
    tab = lambda b: (0, 0)
    whole = lambda b: (0, 0, 0)
    return pl.pallas_call(
        functools.partial(_ret_kernel, nchunk, chunk),
        out_shape=(jax.ShapeDtypeStruct((nb * seq, W), out_dtype),
                   jax.ShapeDtypeStruct((nb, H, D, D), F32)),
        grid=(nb,),
        in_specs=[pl.BlockSpec((seq, W), col(0)), pl.BlockSpec((seq, W), col(1)),
                  pl.BlockSpec((seq, W), col(2)), pl.BlockSpec((seq, W), col(3)),
                  pl.BlockSpec((seq, D), tab), pl.BlockSpec((seq, D), tab),
                  pl.BlockSpec((H, chunk, chunk), whole), pl.BlockSpec((H, chunk, 1), whole),
                  pl.BlockSpec((H, chunk, 1), whole), pl.BlockSpec((H, 1, 1), whole),
                  pl.BlockSpec((1, W), tab)],
        out_specs=(pl.BlockSpec((seq, W), lambda b: (b, 0)),
                   pl.BlockSpec((1, H, D, D), lambda b: (b, 0, 0, 0))),
        compiler_params=_params(("parallel",)),
        name="retention_prompt",
    )(z, z, z, z, cos, sin, decay, xi, zeta, g_c, gn)


def _causal_conv3_zero_prefix(u, w):
    row = lax.broadcasted_iota(I32, u.shape, 0)
    u1 = jnp.where(row >= 1, pltpu.roll(u, 1, axis=0), 0.0)
    u2 = jnp.where(row >= 2, pltpu.roll(u, 2, axis=0), 0.0)
    return w[0:1, :] * u2 + w[1:2, :] * u1 + w[2:3, :] * u


def _conv_kernel(seq, cb_ref, cc_ref, ch_ref, w_ref, o_ref, st_ref):
    u = cc_ref[...] * ch_ref[...]
    o_ref[...] = (cb_ref[...] * _causal_conv3_zero_prefix(u, w_ref[...])).astype(o_ref.dtype)
    st_ref[0] = u[seq - (CONV_K - 1):, :]


def _conv_prompt(z, w, nb, seq, *, out_dtype=BF16):
    C = CONV_WIDTH
    c0 = _Z_CONV // C
    col = lambda c: (lambda b: (b, c0 + c))
    return pl.pallas_call(
        functools.partial(_conv_kernel, seq),
        out_shape=(jax.ShapeDtypeStruct((nb * seq, C), out_dtype),
                   jax.ShapeDtypeStruct((nb, CONV_K - 1, C), F32)),
        grid=(nb,),
        in_specs=[pl.BlockSpec((seq, C), col(0)), pl.BlockSpec((seq, C), col(1)),
                  pl.BlockSpec((seq, C), col(2)), pl.BlockSpec((CONV_K, C), lambda b: (0, 0))],
        out_specs=(pl.BlockSpec((seq, C), lambda b: (b, 0)),
                   pl.BlockSpec((1, CONV_K - 1, C), lambda b: (b, 0, 0))),
        compiler_params=_params(("parallel",)),
        name="conv_prompt",
    )(z, z, z, w)


_HALO = 16


def _upgate_kernel(tiles_per_seq, x_ref, halo_ref, wa_ref, wb_ref, ca_ref, cb_ref, o_ref, st_ref):
    i = pl.program_id(0)
    tm, tf = x_ref.shape[0], wa_ref.shape[1]
    x, xh = x_ref[...], halo_ref[...]
    seq_start = (i % tiles_per_seq) == 0
    row = lax.broadcasted_iota(I32, (tm, tf), 0)

    def conv_branch(part, w_ref, taps_ref):
        h = _dot(x, w_ref[...])
        hh = _dot(xh, w_ref[...])
        p1 = jnp.where(seq_start, 0.0, hh[_HALO - 1:_HALO, :])
        p2 = jnp.where(seq_start, 0.0, hh[_HALO - 2:_HALO - 1, :])
        u1 = jnp.where(row >= 1, pltpu.roll(h, 1, axis=0), p1)
        u2 = jnp.where(row >= 2, pltpu.roll(h, 2, axis=0), jnp.where(row == 1, p1, p2))
        st_ref[0, part] = h[tm - (FFN_K - 1):, :]
        taps = taps_ref[...]
        return taps[0:1, :] * u2 + taps[1:2, :] * u1 + taps[2:3, :] * h

    a = conv_branch(0, wa_ref, ca_ref)
    b = conv_branch(1, wb_ref, cb_ref)
    o_ref[...] = (_silu(a) * b).astype(o_ref.dtype)


def _upgate_prompt(x_lo, w_tiles, taps, layer, nb, seq, *, tm_pref=1024):
    M, D = x_lo.shape
    tf = w_tiles.shape[3]
    nf = w_tiles.shape[1] // 2
    F = nf * tf
    tm = _tile(seq, tm_pref, _HALO)
    tps = seq // tm
    hb = tm // _HALO
    gated, tails = pl.pallas_call(
        functools.partial(_upgate_kernel, tps),
        out_shape=(jax.ShapeDtypeStruct((M, F), x_lo.dtype),
                   jax.ShapeDtypeStruct((M // tm, 2, FFN_K - 1, F), F32)),
        grid=(M // tm, nf),
        in_specs=[pl.BlockSpec((tm, D), lambda i, j: (i, 0)),
                  pl.BlockSpec((_HALO, D), lambda i, j: (jnp.maximum(i * hb - 1, 0), 0)),
                  pl.BlockSpec((None, None, D, tf), lambda i, j: (layer, j, 0, 0)),
                  pl.BlockSpec((None, None, D, tf), lambda i, j: (layer, nf + j, 0, 0)),
                  pl.BlockSpec((FFN_K, tf), lambda i, j: (0, j)),
                  pl.BlockSpec((FFN_K, tf), lambda i, j: (0, nf + j))],
        out_specs=(pl.BlockSpec((tm, tf), lambda i, j: (i, j)),
                   pl.BlockSpec((1, 2, FFN_K - 1, tf), lambda i, j: (i, 0, 0, j))),
        compiler_params=_params(("parallel", "arbitrary")),
        name="upgate_prompt",
    )(x_lo, x_lo, w_tiles, w_tiles, taps, taps)
    state = tails[tps - 1::tps].transpose(0, 2, 1, 3).reshape(nb, FFN_K - 1, 2 * F)
    return gated, state


def _sample_mix_kernel(nb, z_ref, st_ref, cos_ref, sin_ref, gam_ref, gn_ref, p0_ref, p1_ref, cw_ref,
                       ro_ref, nst_ref, co_ref, u_ref):
    D = RET_DIM
    rowi = lax.broadcasted_iota(I32, (nb, D), 0)
    di = lax.broadcasted_iota(I32, (D, D), 0)
    dj = lax.broadcasted_iota(I32, (D, D), 1)
    cos, sin = cos_ref[...], sin_ref[...]
    for h in range(RET_HEADS):
        sl = lambda c: slice((c * RET_HEADS + h) * D, (c * RET_HEADS + h + 1) * D)
        q, k, v, g = z_ref[:, sl(0)], z_ref[:, sl(1)], z_ref[:, sl(2)], z_ref[:, sl(3)]
        gam = gam_ref[:, h * D:(h + 1) * D]
        qr = q * cos + pltpu.roll(q, D // 2, axis=1) * sin
        kr = (k * cos + pltpu.roll(k, D // 2, axis=1) * sin) * (D ** -0.5)
        qf, kf, vf = (t.astype(BF16).astype(F32) for t in (qr, kr, v))
        inner = jnp.sum(qf * kf, axis=-1, keepdims=True)
        o = inner.astype(BF16).astype(F32) * vf
        cross = jnp.zeros((nb, D), F32)
        for b in range(nb):
            R = st_ref[b, h]
            cross = cross + jnp.where(rowi == b, _dot(qf.astype(BF16), R.astype(BF16)), 0.0)
            kdiag = jnp.where(di == dj, jnp.broadcast_to(kf[b:b + 1, :], (D, D)), 0.0).astype(BF16)
            vrows = jnp.broadcast_to(vf[b:b + 1, :], (D, D)).astype(BF16)
            nst_ref[b, h] = R * gam + _dot(kdiag, vrows)
        o = o + cross * gam
        mu = jnp.mean(o, axis=-1, keepdims=True)
        d = o - mu
        var = jnp.mean(d * d, axis=-1, keepdims=True)
        on = d * lax.rsqrt(var + LN_EPS) * gn_ref[:, h * D:(h + 1) * D]
        ro_ref[:, h * D:(h + 1) * D] = on * _silu(g)
    C = CONV_WIDTH
    cb, cc, ch = (z_ref[:, _Z_CONV + c * C:_Z_CONV + (c + 1) * C] for c in range(3))
    u = cc * ch
    w = cw_ref[...]
    co_ref[...] = cb * (w[0:1, :] * p0_ref[...] + w[1:2, :] * p1_ref[...] + w[2:3, :] * u)
    u_ref[...] = u


def _sample_mix(z, state, cos, sin, gam, gn, p0, p1, cw):
    nb = z.shape[0]
    D, H, C = RET_DIM, RET_HEADS, CONV_WIDTH
    return pl.pallas_call(
        functools.partial(_sample_mix_kernel, nb),
        out_shape=(jax.ShapeDtypeStruct((nb, RET_WIDTH), F32), jax.ShapeDtypeStruct((nb, H, D, D), F32),
                   jax.ShapeDtypeStruct((nb, C), F32), jax.ShapeDtypeStruct((nb, C), F32)),
        name="sample_mix",
        compiler_params=pltpu.CompilerParams(vmem_limit_bytes=_VMEM_LIMIT),
    )(z, state, cos, sin, gam, gn, p0, p1, cw)


def _sample_gate_kernel(F, h_ref, p0_ref, p1_ref, w_ref, o_ref):
    w = w_ref[...]
    y = w[0:1, :] * p0_ref[...] + w[1:2, :] * p1_ref[...] + w[2:3, :] * h_ref[...]
    o_ref[...] = _silu(y[:, :F]) * y[:, F:]


def _sample_gate(h, p0, p1, w):
    nb, F2 = h.shape
    return pl.pallas_call(
        functools.partial(_sample_gate_kernel, F2 // 2),
        out_shape=jax.ShapeDtypeStruct((nb, F2 // 2), F32),
        name="sample_gate",
    )(h, p0, p1, w)


def _sample_score_kernel(layer, topk, nb, n_pages, group, pt_ref, iq_ref, w_ref, ikn_ref, cache_ref,
                         info_ref, buf, sem, key_sc, new_sc):
    P = n_pages

    def page_copy(b, p):
        return pltpu.make_async_copy(cache_ref.at[layer, pt_ref[b, p]], buf.at[b * P + p], sem.at[b])

    for b in range(nb):
        def start(p, c, b=b):
            page_copy(b, p).start()
            return c
        lax.fori_loop(0, P, start, 0)

    half = nb // 2 if nb % 2 == 0 else nb
    for b0 in range(0, nb, half):
        tokens = range(b0, b0 + half)
        for b in tokens:
            def wait(p, c, b=b):
                page_copy(b, p).wait()
                return c
            lax.fori_loop(0, P, wait, 0)

        iqs = [iq_ref[b] for b in tokens]
        ws = [w_ref[b] for b in tokens]

        def score_group(g, carry, tokens=tokens, iqs=iqs, ws=ws):
            p0 = g * group
            for b, iq, w in zip(tokens, iqs, ws):
                ik_past = jnp.concatenate([buf[b * P + p0 + pp] for pp in range(group)], axis=1)
                d = _dot(iq, ik_past.astype(iq.dtype))
                s = _float_key(jnp.sum(jnp.maximum(d, 0.0) * w, axis=0, keepdims=True))
                for pp in range(group):
                    key_sc[b, pl.ds(p0 + pp, 1), :] = s[:, pp * PAGE_SIZE:(pp + 1) * PAGE_SIZE]
            return carry

        lax.fori_loop(0, P // group, score_group, 0)
        for b, iq, w in zip(tokens, iqs, ws):
            dn = jnp.sum(iq.astype(F32) * ikn_ref[b].astype(iq.dtype).astype(F32), axis=-1, keepdims=True)
            s_new = jnp.sum(jnp.maximum(dn, 0.0) * w, axis=0, keepdims=True)
            new_sc[b] = jnp.broadcast_to(_float_key(s_new), (8, LANE))

    keys = key_sc[...]
    key_new = new_sc[...][:, 0:1, 0:1]

    def total(x):
        part = jnp.sum(x.reshape(nb, P // 8, 8, LANE), axis=1)
        return jnp.sum(jnp.sum(part, axis=2, keepdims=True), axis=1, keepdims=True)

    def count_ge(cand):
        return total(jnp.where(keys >= cand, 1.0, 0.0)) + jnp.where(key_new >= cand, 1.0, 0.0)

    kf = float(topk)
    t0 = jnp.where(count_ge(jnp.zeros((nb, 1, 1), I32)) >= kf, 0, _INT_MIN).astype(I32)

    def bit_step(it, t):
        cand = t | lax.shift_left(jnp.int32(1), jnp.int32(30) - it)
        return jnp.where(count_ge(cand) >= kf, cand, t)

    thr = lax.fori_loop(0, 31, bit_step, t0)
    need = kf - total(jnp.where(keys > thr, 1.0, 0.0)) - jnp.where(key_new > thr, 1.0, 0.0)

    li = lax.broadcasted_iota(I32, (LANE, LANE), 0)
    lj = lax.broadcasted_iota(I32, (LANE, LANE), 1)
    upper = jnp.where(li <= lj, 1.0, 0.0).astype(BF16)
    pi = lax.broadcasted_iota(I32, (P, P), 0)
    pj = lax.broadcasted_iota(I32, (P, P), 1)
    lower_strict = jnp.where(pj < pi, 1.0, 0.0).astype(BF16)
    upper_pages = jnp.where(pi <= pj, 1.0, 0.0).astype(BF16)
    ones8 = jnp.ones((8, LANE), BF16)
    slot = lax.broadcasted_iota(I32, (topk, P), 0).astype(F32)
    page_id = lax.broadcasted_iota(I32, (topk, P), 1).astype(F32)
    out_lane = lax.broadcasted_iota(I32, (topk, LANE), 1)

    for b in range(nb):
        kb, tb, nd = keys[b], thr[b], need[b]
        eq = jnp.where(kb == tb, 1.0, 0.0)
        eq_incl = _dot(eq.astype(BF16), upper)
        eq_before = _dot(lower_strict, jnp.broadcast_to(eq_incl[:, LANE - 1:LANE], eq.shape).astype(BF16))
        eq_rank = eq_before + eq_incl - eq
        sel = jnp.where(kb > tb, 1.0, eq * jnp.where(eq_rank < nd, 1.0, 0.0))
        sel_lo = sel.astype(BF16)
        within = _dot(sel_lo, upper)
        cnt_row = _dot_nt(ones8, sel_lo)
        incl_row = _dot(cnt_row.astype(BF16), upper_pages)[0:1, :]
        n_sel = incl_row[:, P - 1:P]
        done = jnp.where(incl_row <= slot[:, 0:1], 1.0, 0.0)
        page = jnp.sum(done, axis=1, keepdims=True)
        before = jnp.sum(done * cnt_row[0:1, :], axis=1, keepdims=True)
        r = slot[:, 0:1] - before
        pick = jnp.where(page_id == page, 1.0, 0.0).astype(BF16)
        within_slot = _dot(pick, within.astype(BF16))
        off = jnp.sum(jnp.where(within_slot <= r, 1.0, 0.0), axis=1, keepdims=True)
        filled = slot[:, 0:1] < n_sel
        info = jnp.where(out_lane == 0, off, jnp.where(out_lane == 1, page, 1.0))
        info = jnp.where(filled & (out_lane < 3), info, 0.0)
        info_ref[b] = info.astype(I32)


def _sample_score(page_table, iq, w, ikn, cache_idx_k, layer, topk):
    nb, n_pages = page_table.shape
    group = _tile(n_pages, 8, 1)
    whole = lambda i, pt: (0, 0, 0)
    grid_spec = pltpu.PrefetchScalarGridSpec(
        num_scalar_prefetch=1,
        grid=(1,),
        in_specs=[pl.BlockSpec((nb, IDX_HEADS, IDX_DIM), whole),
                  pl.BlockSpec((nb, IDX_HEADS, 1), whole),
                  pl.BlockSpec((nb, 1, IDX_DIM), whole),
                  pl.BlockSpec(memory_space=pl.ANY)],
        out_specs=pl.BlockSpec((nb, topk, LANE), whole),
        scratch_shapes=[pltpu.VMEM((nb * n_pages, IDX_DIM, PAGE_SIZE), F32),
                        pltpu.SemaphoreType.DMA((nb,)),
                        pltpu.VMEM((nb, n_pages, PAGE_SIZE), I32),
                        pltpu.VMEM((nb, 8, LANE), I32)],
    )
    return pl.pallas_call(
        functools.partial(_sample_score_kernel, layer, topk, nb, n_pages, group),
        out_shape=jax.ShapeDtypeStruct((nb, topk, LANE), I32),
        grid_spec=grid_spec,
        compiler_params=_params(("arbitrary",)),
        name="sample_score",
    )(page_table, iq, w, ikn, cache_idx_k)


def _sample_attn_kernel(layer, topk, pt_ref, page_ref, off_ref, q_ref, kn_ref, vn_ref, inpast_ref,
                        ck_ref, cv_ref, o_ref, kbuf, vbuf, sem):
    b = pl.program_id(0)
    nb = pl.num_programs(0)
    slot = b % 2

    def copies(bb, sl, j):
        src = (layer, pt_ref[bb, page_ref[bb, j]], off_ref[bb, j])
        return (pltpu.make_async_copy(ck_ref.at[src], kbuf.at[sl, j], sem.at[0, sl]),
                pltpu.make_async_copy(cv_ref.at[src], vbuf.at[sl, j], sem.at[1, sl]))

    def start_rows(bb, sl):
        def start(j, c):
            ck, cv = copies(bb, sl, j)
            ck.start()
            cv.start()
            return c
        lax.fori_loop(0, topk, start, 0)

    @pl.when(b == 0)
    def _():
        start_rows(b, slot)

    @pl.when(b + 1 < nb)
    def _():
        start_rows(b + 1, 1 - slot)

    def wait(j, c):
        ck, cv = copies(b, slot, j)
        ck.wait()
        cv.wait()
        return c

    lax.fori_loop(0, topk, wait, 0)

    inpast = inpast_ref[0] > 0
    ks = jnp.where(inpast, kbuf[slot], kn_ref[...])
    vs = jnp.where(inpast, vbuf[slot], vn_ref[...])
    q = q_ref[...].astype(BF16).astype(F32)
    s = jnp.sum(q * ks.astype(BF16).astype(F32), axis=-1, keepdims=True) * ATT_SCALE
    m = jnp.max(s, axis=0, keepdims=True)
    e = jnp.exp(s - m)
    p = e / jnp.sum(e, axis=0, keepdims=True)
    o_ref[...] = jnp.sum(p.astype(BF16).astype(F32) * vs.astype(BF16).astype(F32), axis=0, keepdims=True)


def _sample_attn(page_table, page, off, q, kn, vn, inpast, cache_k, cache_v, layer, topk):
    nb = q.shape[0]
    H, D = ATT_HEADS, ATT_DIM
    per_b = lambda b, pt, pg, of: (b, 0, 0)
    grid_spec = pltpu.PrefetchScalarGridSpec(
        num_scalar_prefetch=3,
        grid=(nb,),
        in_specs=[pl.BlockSpec((1, H, D), per_b), pl.BlockSpec((1, H, D), per_b),
                  pl.BlockSpec((1, H, D), per_b),
                  pl.BlockSpec((1, topk, 1, 1), lambda b, pt, pg, of: (b, 0, 0, 0)),
                  pl.BlockSpec(memory_space=pl.ANY), pl.BlockSpec(memory_space=pl.ANY)],
        out_specs=pl.BlockSpec((1, H, D), per_b),
        scratch_shapes=[pltpu.VMEM((2, topk, H, D), F32), pltpu.VMEM((2, topk, H, D), F32),
                        pltpu.SemaphoreType.DMA((2, 2))],
    )
    return pl.pallas_call(
        functools.partial(_sample_attn_kernel, layer, topk),
        out_shape=jax.ShapeDtypeStruct((nb, H, D), F32),
        grid_spec=grid_spec,
        compiler_params=_params(("arbitrary",)),
        name="sample_attn",
    )(page_table, page, off, q, kn, vn, inpast, cache_k, cache_v)


def kernel(x_prompt, x_sample, cache_k, cache_v, cache_idx_k, state_ret, state_conv, state_ffn,
           page_table, w_in, ret_gn_g, idx_kn_g, idx_kn_b, conv_w, w_out, ln1_g, ln1_b,
           w_up, ffn_conv_w, w_down, ln2_g, ln2_b):
    B, S, D = x_prompt.shape
    DB, T, _ = x_sample.shape
    assert T == 1 and S % PAGE_SIZE == 0 and w_in.shape[-1] == _D_IN
    depth = w_in.shape[0]
    n_pages = page_table.shape[1]
    past = n_pages * PAGE_SIZE
    F2 = w_up.shape[-1]
    alpha = (2 * depth) ** 0.25
    topk_s = min(TOPK_MAX, (past + T) // 4)

    pos_p = jnp.arange(S, dtype=jnp.int32)
    pos_s = past + jnp.arange(T, dtype=jnp.int32)
    att_tabs_p = _rope_tables(pos_p, ROT_DIM, ROPE_THETA, ATT_DIM)
    idx_tabs_p = _rope_tables(pos_p, IDX_ROT_DIM, ROPE_THETA, IDX_DIM)
    rc, ru, rd = _rope_tables(pos_p, RET_DIM, RET_THETA, RET_DIM)
    ret_cos_p, ret_sin_p = rc, ru + rd
    bcast = lambda t: jnp.broadcast_to(t, (DB, LANE))
    att_tabs_s = tuple(bcast(t) for t in _rope_tables(pos_s, ROT_DIM, ROPE_THETA, ATT_DIM))
    idx_tabs_s = tuple(bcast(t) for t in _rope_tables(pos_s, IDX_ROT_DIM, ROPE_THETA, IDX_DIM))
    rc, ru, rd = _rope_tables(pos_s, RET_DIM, RET_THETA, RET_DIM)
    ret_cos_s, ret_sin_s = bcast(rc), bcast(ru + rd)
    gamma = 1.0 - 2.0 ** (-5.0 - jnp.arange(RET_HEADS, dtype=F32))
    gamma = jnp.exp(jnp.log(gamma))
    gam_row = jnp.repeat(gamma, RET_DIM)[None, :]

    cache_idx_t = jnp.swapaxes(cache_idx_k, 2, 3)
    xp = x_prompt.reshape(B * S, D)
    xp_lo = xp.astype(BF16)
    xs = x_sample.reshape(DB, D)

    outs_p = [[] for _ in range(6)]
    outs_s = [[] for _ in range(6)]
    row2 = lambda v: v.reshape(1, -1)
    pad_small = lambda v: jnp.pad(v, (0, _SMALL_W - v.shape[0])).reshape(1, _SMALL_W)

    w_main = jnp.concatenate([w_in[:, :, :_O_IK], w_in[:, :, _O_CONV:]], axis=2).astype(BF16)
    w_main = _col_tiles(w_main, _tile(_Z_MAIN, _PROJ_TN, LANE))
    w_small = jnp.pad(w_in[:, :, _O_IK:_O_CONV],
                      ((0, 0), (0, 0), (0, _SMALL_W - (_O_CONV - _O_IK)))).astype(BF16)
    w_out_lo, w_down_lo = w_out.astype(BF16), w_down.astype(BF16)
    w_up_lo = _col_tiles(w_up.astype(BF16), _tile(F2 // 2, _PROJ_TN, LANE))

    for l in range(depth):
        gn = row2(ret_gn_g[l])
        kg, kb_ = pad_small(idx_kn_g[l]), pad_small(idx_kn_b[l])
        g1, b1, g2, b2 = row2(ln1_g[l]), row2(ln1_b[l]), row2(ln2_g[l]), row2(ln2_b[l])

        z, zs = _inproj(xp_lo, w_main, w_small, l)
        ro, p_ret = _retention_prompt(z, ret_cos_p, ret_sin_p, gn, B, S)
        qb, kb, vT, kf, iqb, ikb, smT = _prep(z, zs, att_tabs_p, idx_tabs_p, kg, kb_, S, transposed=True)
        ao = _dsa_prompt(iqb, ikb, smT, qb, kb, vT, B, S)
        co, p_conv = _conv_prompt(z, conv_w[l], B, S)
        x1, x1_lo = _outproj_ln(ro, ao, co, w_out_lo, l, xp, g1, b1, alpha)
        gt, p_ffn = _upgate_prompt(x1_lo, w_up_lo, ffn_conv_w[l], l, B, S)
        xp, xp_lo = _down_ln(gt, w_down_lo, l, x1, g2, b2, alpha, tk_pref=1408)
        outs_p[0].append(kf.reshape(B, S, ATT_HEADS, ATT_DIM))
        outs_p[1].append(z[:, _O_ATT + 2 * ATT_WIDTH:_O_ATT + 3 * ATT_WIDTH].reshape(B, S, ATT_HEADS, ATT_DIM))
        outs_p[2].append(smT[:IDX_DIM, :].T.reshape(B, S, IDX_DIM))
        outs_p[3].append(p_ret)
        outs_p[4].append(p_conv)
        outs_p[5].append(p_ffn)

        z, zs = _inproj(xs, w_main, w_small, l)
        ro, s_ret, co, u_new = _sample_mix(z, state_ret[l], ret_cos_s, ret_sin_s, gam_row, gn,
                                           state_conv[l, :, 0], state_conv[l, :, 1], conv_w[l])
        q_s, _, _, k_s, iq_s, _, sm_s = _prep(z, zs, att_tabs_s, idx_tabs_s, kg, kb_, DB,
                                              transposed=False, lowp_dtype=F32)
        v_s = z[:, _O_ATT + 2 * ATT_WIDTH:_O_ATT + 3 * ATT_WIDTH]
        ik_s = sm_s[:, :IDX_DIM]
        iw_s = sm_s[:, IDX_DIM:IDX_DIM + IDX_HEADS]
        info = _sample_score(page_table, iq_s.reshape(DB, IDX_HEADS, IDX_DIM).astype(BF16),
                             iw_s.reshape(DB, IDX_HEADS, 1), ik_s.reshape(DB, 1, IDX_DIM),
                             cache_idx_t, l, topk_s)
        hd = lambda t: t.reshape(DB, ATT_HEADS, ATT_DIM)
        ao = _sample_attn(page_table, info[:, :, 1], info[:, :, 0], hd(q_s), hd(k_s), hd(v_s),
                          info[:, :, 2].reshape(DB, topk_s, 1, 1), cache_k, cache_v, l, topk_s)
        x1, _ = _outproj_ln(ro, ao.reshape(DB, ATT_WIDTH), co, w_out_lo, l, xs, g1, b1, alpha,
                            lowp_dtype=F32)
        h = _matmul(x1, w_up_lo, l, name="up_s")
        gt = _sample_gate(h, state_ffn[l, :, 0], state_ffn[l, :, 1], ffn_conv_w[l])
        xs_new, _ = _down_ln(gt, w_down_lo, l, x1, g2, b2, alpha, lowp_dtype=F32)
        outs_s[0].append(k_s.reshape(DB, T, ATT_HEADS, ATT_DIM))
        outs_s[1].append(v_s.reshape(DB, T, ATT_HEADS, ATT_DIM))
        outs_s[2].append(ik_s.reshape(DB, T, IDX_DIM))
        outs_s[3].append(s_ret)
        outs_s[4].append(jnp.stack([state_conv[l, :, 1], u_new], axis=1))
        outs_s[5].append(jnp.stack([state_ffn[l, :, 1], h], axis=1))
        xs = xs_new

    n_pp = S // PAGE_SIZE
    st = lambda xs_: jnp.stack(xs_)
    return (xp.reshape(B, S, D), xs.reshape(DB, T, D),
            st(outs_p[0]).reshape(depth, B, n_pp, PAGE_SIZE, ATT_HEADS, ATT_DIM),
            st(outs_p[1]).reshape(depth, B, n_pp, PAGE_SIZE, ATT_HEADS, ATT_DIM),
            st(outs_p[2]).reshape(depth, B, n_pp, PAGE_SIZE, IDX_DIM),
            st(outs_p[3]), st(outs_p[4]), st(outs_p[5]),
            st(outs_s[0]), st(outs_s[1]), st(outs_s[2]), st(outs_s[3]), st(outs_s[4]), st(outs_s[5]))
```
